```python
import jax, jax.numpy as jnp
from jax import lax
import numpy as np

D_MODEL = 1024
BATCH = 32
SEQ = 2048
DEPTH = 2

N_Q_HEADS = 8
N_KV_HEADS = 2
HEAD_DIM = 64
WINDOW = 128
ATTN_BLOCK = 128
ATTN_WIDTH = N_Q_HEADS * HEAD_DIM
KV_WIDTH = N_KV_HEADS * HEAD_DIM
CONV_WIDTH = D_MODEL - ATTN_WIDTH
CONV_KERNEL = 31
IN0_WIDTH = ATTN_WIDTH + 2 * KV_WIDTH + 2 * CONV_WIDTH
POOL_WINDOWS = (2, 4, 8, 16)
POOL_WIDTH = D_MODEL // 2
POOL_GROUP = POOL_WIDTH // len(POOL_WINDOWS)
SGU_WIDTH = D_MODEL - POOL_WIDTH
SGU_HEADS = 4
SGU_HEAD_DIM = SGU_WIDTH // SGU_HEADS
SGU_CHUNK = 128
IN1_WIDTH = POOL_WIDTH + 2 * SGU_WIDTH
D_FF = -(-(8 * D_MODEL) // (3 * 256)) * 256
N_EVEN = (DEPTH + 1) // 2
N_ODD = DEPTH // 2
EPS = 1e-5

kernel_name = "hybrid_swa_conformer_pool_sgu"


def rms_norm(x, g):
    xf = x.astype(jnp.float32)
    y = xf * lax.rsqrt(jnp.mean(xf * xf, axis=-1, keepdims=True) + EPS)
    return (y * g.astype(jnp.float32)).astype(x.dtype)


def layer_norm(x, g, b):
    xf = x.astype(jnp.float32)
    mu = jnp.mean(xf, axis=-1, keepdims=True)
    var = jnp.mean(jnp.square(xf - mu), axis=-1, keepdims=True)
    y = (xf - mu) * lax.rsqrt(var + EPS)
    return (y * g.astype(jnp.float32) + b.astype(jnp.float32)).astype(x.dtype)


def sliding_window_attention(q, k, v, sinks):
    B, S = q.shape[0], q.shape[1]
    nb = S // ATTN_BLOCK
    G = N_Q_HEADS // N_KV_HEADS
    qb = q.reshape(B, nb, ATTN_BLOCK, N_KV_HEADS, G, HEAD_DIM)
    kb = k.reshape(B, nb, ATTN_BLOCK, N_KV_HEADS, HEAD_DIM)
    vb = v.reshape(B, nb, ATTN_BLOCK, N_KV_HEADS, HEAD_DIM)

    def with_prev(t):
        prev = jnp.pad(t, ((0, 0), (1, 0), (0, 0), (0, 0), (0, 0)))[:, :-1]
        return jnp.concatenate([prev, t], axis=2)

    kw, vw = with_prev(kb), with_prev(vb)
    logits = jnp.einsum('bnqkgd,bnskd->bnkgqs', qb, kw).astype(jnp.float32) * (HEAD_DIM ** -0.5)
    qi = jnp.arange(ATTN_BLOCK)[:, None]
    r = jnp.arange(2 * ATTN_BLOCK)[None, :]
    dist = qi + ATTN_BLOCK - r
    band = (dist >= 0) & (dist < WINDOW)
    key_pos = jnp.arange(nb)[:, None, None] * ATTN_BLOCK + r[None] - ATTN_BLOCK
    mask = band[None] & (key_pos >= 0)
    logits = jnp.where(mask[None, :, None, None], logits, -jnp.inf)
    sink = sinks.astype(jnp.float32).reshape(1, 1, N_KV_HEADS, G, 1, 1)
    m = jnp.maximum(jnp.max(logits, axis=-1, keepdims=True), sink)
    p = jnp.exp(logits - m)
    probs = p / (jnp.sum(p, axis=-1, keepdims=True) + jnp.exp(sink - m))
    out = jnp.einsum('bnkgqs,bnskd->bnqkgd', probs.astype(v.dtype), vw)
    return out.reshape(B, S, ATTN_WIDTH)


def conformer_conv(c, conv_w, conv_b, ln_g, ln_b):
    a, gate = jnp.split(c, 2, axis=-1)
    h = a * jax.nn.sigmoid(gate)
    h = lax.conv_general_dilated(
        h, conv_w[:, None, :].astype(h.dtype), window_strides=(1,),
        padding=[(CONV_KERNEL - 1, 0)],
        dimension_numbers=('NWC', 'WIO', 'NWC'),
        feature_group_count=CONV_WIDTH) + conv_b
    h = layer_norm(h, ln_g, ln_b)
    return jax.nn.silu(h)


def multiscale_pool(z, w_pool, scale):
    S = z.shape[1]
    zf = z.astype(jnp.float32)
    cs = jnp.cumsum(zf, axis=1)
    t = jnp.arange(S)
    outs = []
    for g, w in enumerate(POOL_WINDOWS):
        lo, hi = g * POOL_GROUP, (g + 1) * POOL_GROUP
        c = cs[..., lo:hi]
        prev = jnp.pad(c, ((0, 0), (w, 0), (0, 0)))[:, :S]
        cnt = jnp.minimum(t + 1, w).astype(jnp.float32)[:, None]
        pooled = (c - prev) / cnt - zf[..., lo:hi]
        outs.append(jnp.einsum('bsc,cd->bsd', pooled.astype(z.dtype), w_pool[g]))
    return jnp.concatenate(outs, axis=-1) * scale


def chunked_spatial_gating(z, ln_g, ln_b, w_s, b_s):
    B, S = z.shape[0], z.shape[1]
    u, v = jnp.split(jax.nn.gelu(z), 2, axis=-1)
    v = layer_norm(v, ln_g, ln_b)
    nc = S // SGU_CHUNK
    vc = v.reshape(B, nc, SGU_CHUNK, SGU_HEADS, SGU_HEAD_DIM)
    causal = jnp.tril(jnp.ones((SGU_CHUNK, SGU_CHUNK), dtype=bool))
    w = jnp.where(causal[None], w_s, jnp.zeros_like(w_s))
    mixed = jnp.einsum('gts,bcsgh->bctgh', w, vc) + b_s.T[None, None, :, :, None]
    return u * mixed.reshape(B, S, SGU_WIDTH)


def swiglu(h, w_gate, w_up, w_down):
    return (jax.nn.silu(h @ w_gate) * (h @ w_up)) @ w_down


def setup_inputs(seed: int = 0) -> dict:
    key = jax.random.key(seed)
    ks = jax.random.split(key, 24)
    f32 = jnp.float32

    def nrm(k, shape, s):
        return jax.random.normal(k, shape, f32) * s

    def gain(k, shape):
        return 1.0 + 0.02 * jax.random.normal(k, shape, f32)

    return {
        'x': jax.random.normal(ks[0], (BATCH, SEQ, D_MODEL), f32),
        'mix_norm': gain(ks[1], (DEPTH, D_MODEL)),
        'a_w_in': nrm(ks[2], (N_EVEN, D_MODEL, IN0_WIDTH), D_MODEL ** -0.5),
        'a_b_in': nrm(ks[3], (N_EVEN, IN0_WIDTH), 0.02),
        'a_sinks': nrm(ks[4], (N_EVEN, N_Q_HEADS), 1.0),
        'a_conv_w': nrm(ks[5], (N_EVEN, CONV_KERNEL, CONV_WIDTH), CONV_KERNEL ** -0.5),
        'a_conv_b': nrm(ks[6], (N_EVEN, CONV_WIDTH), 0.02),
        'a_cln_g': gain(ks[7], (N_EVEN, CONV_WIDTH)),
        'a_cln_b': nrm(ks[8], (N_EVEN, CONV_WIDTH), 0.02),
        'a_w_out': nrm(ks[9], (N_EVEN, D_MODEL, D_MODEL), D_MODEL ** -0.5),
        'c_w_in': nrm(ks[10], (N_ODD, D_MODEL, IN1_WIDTH), D_MODEL ** -0.5),
        'c_w_pool': nrm(ks[11], (N_ODD, len(POOL_WINDOWS), POOL_GROUP, POOL_GROUP), POOL_GROUP ** -0.5),
        'c_pool_scale': gain(ks[12], (N_ODD, POOL_WIDTH)),
        'c_sln_g': gain(ks[13], (N_ODD, SGU_WIDTH)),
        'c_sln_b': nrm(ks[14], (N_ODD, SGU_WIDTH), 0.02),
        'c_w_s': nrm(ks[15], (N_ODD, SGU_HEADS, SGU_CHUNK, SGU_CHUNK), SGU_CHUNK ** -0.5),
        'c_b_s': gain(ks[16], (N_ODD, SGU_HEADS, SGU_CHUNK)),
        'c_w_out': nrm(ks[17], (N_ODD, D_MODEL, D_MODEL), D_MODEL ** -0.5),
        'ffn_norm': gain(ks[18], (DEPTH, D_MODEL)),
        'ffn_w_gate': nrm(ks[19], (DEPTH, D_MODEL, D_FF), D_MODEL ** -0.5),
        'ffn_w_up': nrm(ks[20], (DEPTH, D_MODEL, D_FF), D_MODEL ** -0.5),
        'ffn_w_down': nrm(ks[21], (DEPTH, D_FF, D_MODEL), D_FF ** -0.5),
        'final_norm': gain(ks[22], (D_MODEL,)),
    }


def reference(x, mix_norm, a_w_in, a_b_in, a_sinks, a_conv_w, a_conv_b, a_cln_g, a_cln_b, a_w_out,
              c_w_in, c_w_pool, c_pool_scale, c_sln_g, c_sln_b, c_w_s, c_b_s, c_w_out,
              ffn_norm, ffn_w_gate, ffn_w_up, ffn_w_down, final_norm):
    B, S = x.shape[0], x.shape[1]
    h = x
    for i in range(DEPTH):
        j = i // 2
        hn = rms_norm(h, mix_norm[i])
        if i % 2 == 0:
            z = hn @ a_w_in[j] + a_b_in[j]
            q = z[..., :ATTN_WIDTH].reshape(B, S, N_Q_HEADS, HEAD_DIM)
            k = z[..., ATTN_WIDTH:ATTN_WIDTH + KV_WIDTH].reshape(B, S, N_KV_HEADS, HEAD_DIM)
            v = z[..., ATTN_WIDTH + KV_WIDTH:ATTN_WIDTH + 2 * KV_WIDTH].reshape(B, S, N_KV_HEADS, HEAD_DIM)
            c = z[..., ATTN_WIDTH + 2 * KV_WIDTH:]
            attn = sliding_window_attention(q, k, v, a_sinks[j])
            conv = conformer_conv(c, a_conv_w[j], a_conv_b[j], a_cln_g[j], a_cln_b[j])
            h = h + jnp.concatenate([attn, conv], axis=-1) @ a_w_out[j]
        else:
            z = hn @ c_w_in[j]
            pool = multiscale_pool(z[..., :POOL_WIDTH], c_w_pool[j], c_pool_scale[j])
            sgu = chunked_spatial_gating(z[..., POOL_WIDTH:], c_sln_g[j], c_sln_b[j], c_w_s[j], c_b_s[j])
            h = h + jnp.concatenate([pool, sgu], axis=-1) @ c_w_out[j]
        h = h + swiglu(rms_norm(h, ffn_norm[i]), ffn_w_gate[i], ffn_w_up[i], ffn_w_down[i])
    return rms_norm(h, final_norm)
```

```python
import functools

import jax
import jax.numpy as jnp
from jax import lax
from jax.experimental import pallas as pl
from jax.experimental.pallas import tpu as pltpu

F32 = jnp.float32
BF16 = jnp.bfloat16

D_MODEL = 1024
N_Q_HEADS = 8
N_KV_HEADS = 2
HEAD_DIM = 64
Q_GROUP = N_Q_HEADS // N_KV_HEADS
ATTN_BLOCK = 128
ATTN_WIDTH = N_Q_HEADS * HEAD_DIM
KV_WIDTH = N_KV_HEADS * HEAD_DIM
CONV_WIDTH = D_MODEL - ATTN_WIDTH
CONV_KERNEL = 31
CONV_HALO = 32
CONV_ROWS = 64
IN0_WIDTH = ATTN_WIDTH + 2 * KV_WIDTH + 2 * CONV_WIDTH
POOL_WINDOWS = (2, 4, 8, 16)
POOL_WIDTH = 512
POOL_GROUP = POOL_WIDTH // len(POOL_WINDOWS)
POOL_HALO = 16
SGU_WIDTH = 512
SGU_HEADS = 4
SGU_HEAD_DIM = SGU_WIDTH // SGU_HEADS
SGU_CHUNK = 128
IN1_WIDTH = POOL_WIDTH + 2 * SGU_WIDTH
EPS = 1e-5

LANES = 128
SEQ_TILE = 512
FFN_TILE = 512
VMEM_LIMIT = 56 * 1024 * 1024


def _rms(x, g):
    return x * lax.rsqrt(jnp.mean(x * x, axis=-1, keepdims=True) + EPS) * g


def _layer_norm(x, g, b):
    mu = jnp.mean(x, axis=-1, keepdims=True)
    xc = x - mu
    var = jnp.mean(xc * xc, axis=-1, keepdims=True)
    return xc * lax.rsqrt(var + EPS) * g + b


def _sigmoid(x):
    return 1.0 / (1.0 + jnp.exp(-x))


def _mixer0_kernel(x_ref, g_ref, win_ref, bin_ref, sink_ref, cw_ref, cb_ref, lg_ref, lb_ref,
                   wout_ref, o_ref, z_ref, kd_ref, vd_ref, hbuf_ref, cat_ref):
    t = pl.program_id(1)
    tile = x_ref.shape[1]
    blk = ATTN_BLOCK

    x = x_ref[0]
    hn = _rms(x, g_ref[...]).astype(BF16)
    z_ref[...] = jnp.dot(hn, win_ref[...], preferred_element_type=F32) + bin_ref[...]

    @pl.when(t == 0)
    def _():
        kd_ref[:, 0:blk, :] = jnp.zeros((N_KV_HEADS, blk, LANES), BF16)
        vd_ref[:, 0:blk, :] = jnp.zeros((N_KV_HEADS, blk, LANES), BF16)
        hbuf_ref[0:CONV_HALO, :] = jnp.zeros((CONV_HALO, CONV_WIDTH), F32)

    @pl.when(t > 0)
    def _():
        kd_ref[:, 0:blk, :] = kd_ref[:, tile:tile + blk, :]
        vd_ref[:, 0:blk, :] = vd_ref[:, tile:tile + blk, :]
        hbuf_ref[0:CONV_HALO, :] = hbuf_ref[tile:tile + CONV_HALO, :]

    lane = lax.broadcasted_iota(jnp.int32, (tile, LANES), 1)
    lo_t = lane < HEAD_DIM
    zk = z_ref[:, ATTN_WIDTH:ATTN_WIDTH + KV_WIDTH]
    zv = z_ref[:, ATTN_WIDTH + KV_WIDTH:ATTN_WIDTH + 2 * KV_WIDTH]
    zk_sw = pltpu.roll(zk, HEAD_DIM, 1)
    zv_sw = pltpu.roll(zv, HEAD_DIM, 1)
    kd_ref[0, blk:blk + tile, :] = jnp.where(lo_t, zk, zk_sw).astype(BF16)
    kd_ref[1, blk:blk + tile, :] = jnp.where(lo_t, zk_sw, zk).astype(BF16)
    vd_ref[0, blk:blk + tile, :] = jnp.where(lo_t, zv, zv_sw).astype(BF16)
    vd_ref[1, blk:blk + tile, :] = jnp.where(lo_t, zv_sw, zv).astype(BF16)

    rows = Q_GROUP * blk
    qi = lax.broadcasted_iota(jnp.int32, (rows, 2 * blk), 0) % blk
    r = lax.broadcasted_iota(jnp.int32, (rows, 2 * blk), 1)
    band = (r > qi) & (r <= qi + blk)
    band_first = band & ((r >= blk) | (t > 0))
    lo = lax.broadcasted_iota(jnp.int32, (blk, LANES), 1) < HEAD_DIM
    scale = HEAD_DIM ** -0.5

    for b in range(tile // blk):
        rs = slice(b * blk, (b + 1) * blk)
        mask = band_first if b == 0 else band
        for kh in range(N_KV_HEADS):
            parts = []
            for j in range(Q_GROUP // 2):
                c0 = kh * Q_GROUP * HEAD_DIM + j * LANES
                qp = z_ref[rs, c0:c0 + LANES] * scale
                parts.append(jnp.where(lo, qp, 0.0))
                parts.append(jnp.where(lo, 0.0, qp))
            qs = jnp.concatenate(parts, axis=0).astype(BF16)
            kw = kd_ref[kh, b * blk:(b + 2) * blk, :]
            vw = vd_ref[kh, b * blk:(b + 2) * blk, :]
            logits = lax.dot_general(qs, kw, (((1,), (1,)), ((), ())), preferred_element_type=F32)
            logits = jnp.where(mask, logits, -jnp.inf)
            sink = sink_ref[kh]
            m = jnp.maximum(jnp.max(logits, axis=-1, keepdims=True), sink)
            p = jnp.exp(logits - m)
            den = jnp.sum(p, axis=-1, keepdims=True) + jnp.exp(sink - m)
            o = jnp.dot(p.astype(BF16), vw, preferred_element_type=F32) / den
            for j in range(Q_GROUP // 2):
                pair = jnp.where(lo, o[2 * j * blk:(2 * j + 1) * blk], o[(2 * j + 1) * blk:(2 * j + 2) * blk])
                c0 = kh * Q_GROUP * HEAD_DIM + j * LANES
                cat_ref[rs, c0:c0 + LANES] = pair.astype(BF16)

    c0 = ATTN_WIDTH + 2 * KV_WIDTH
    a = z_ref[:, c0:c0 + CONV_WIDTH]
    gate = z_ref[:, c0 + CONV_WIDTH:c0 + 2 * CONV_WIDTH]
    hbuf_ref[CONV_HALO:CONV_HALO + tile, :] = a * _sigmoid(gate)
    first = CONV_HALO - (CONV_KERNEL - 1)
    for c in range(tile // CONV_ROWS):
        base = c * CONV_ROWS + first
        acc = jnp.broadcast_to(cb_ref[...], (CONV_ROWS, CONV_WIDTH))
        for k in range(CONV_KERNEL):
            acc = acc + cw_ref[k:k + 1, :] * hbuf_ref[base + k:base + k + CONV_ROWS, :]
        y = _layer_norm(acc, lg_ref[...], lb_ref[...])
        y = y * _sigmoid(y)
        cat_ref[c * CONV_ROWS:(c + 1) * CONV_ROWS, ATTN_WIDTH:ATTN_WIDTH + CONV_WIDTH] = y.astype(BF16)

    o_ref[0] = x + jnp.dot(cat_ref[...], wout_ref[...], preferred_element_type=F32)


def _full(shape):
    return pl.BlockSpec(shape, lambda *_: (0,) * len(shape))


def _mixer0(x, g, w_in, b_in, sinks, conv_w, conv_b, ln_g, ln_b, w_out):
    B, S, D = x.shape
    tile = SEQ_TILE
    rows = Q_GROUP * ATTN_BLOCK
    sink_cols = jnp.repeat(sinks.reshape(N_KV_HEADS, Q_GROUP), ATTN_BLOCK, axis=1).reshape(N_KV_HEADS, rows, 1)
    return pl.pallas_call(
        _mixer0_kernel,
        name="mixer0",
        grid=(B, S // tile),
        in_specs=[
            pl.BlockSpec((1, tile, D), lambda b, t: (b, t, 0)),
            _full((1, D)),
            _full((D, IN0_WIDTH)),
            _full((1, IN0_WIDTH)),
            _full((N_KV_HEADS, rows, 1)),
            _full((CONV_KERNEL, CONV_WIDTH)),
            _full((1, CONV_WIDTH)),
            _full((1, CONV_WIDTH)),
            _full((1, CONV_WIDTH)),
            _full((D, D)),
        ],
        out_specs=pl.BlockSpec((1, tile, D), lambda b, t: (b, t, 0)),
        out_shape=jax.ShapeDtypeStruct((B, S, D), F32),
        scratch_shapes=[
            pltpu.VMEM((tile, IN0_WIDTH), F32),
            pltpu.VMEM((N_KV_HEADS, ATTN_BLOCK + tile, LANES), BF16),
            pltpu.VMEM((N_KV_HEADS, ATTN_BLOCK + tile, LANES), BF16),
            pltpu.VMEM((CONV_HALO + tile, CONV_WIDTH), F32),
            pltpu.VMEM((tile, D), BF16),
        ],
        compiler_params=pltpu.CompilerParams(
            dimension_semantics=("arbitrary", "arbitrary"), vmem_limit_bytes=VMEM_LIMIT),
    )(x, g.reshape(1, D), w_in.astype(BF16), b_in.reshape(1, IN0_WIDTH), sink_cols, conv_w,
      conv_b.reshape(1, CONV_WIDTH), ln_g.reshape(1, CONV_WIDTH), ln_b.reshape(1, CONV_WIDTH),
      w_out.astype(BF16))


def _gelu_tanh(x):
    c = 0.7978845608028654
    return 0.5 * x * (1.0 + jnp.tanh(c * (x + 0.044715 * (x * x * x))))


def _mixer1_kernel(x_ref, g_ref, win_ref, wpool_ref, pscale_ref, lg_ref, lb_ref, ws_ref, bs_ref,
                   wout_ref, o_ref, z_ref, pbuf_ref, cat_ref):
    t = pl.program_id(1)
    tile = x_ref.shape[1]

    x = x_ref[0]
    hn = _rms(x, g_ref[...]).astype(BF16)
    z_ref[...] = jnp.dot(hn, win_ref[...], preferred_element_type=F32)

    @pl.when(t == 0)
    def _():
        pbuf_ref[0:POOL_HALO, :] = jnp.zeros((POOL_HALO, POOL_WIDTH), F32)

    @pl.when(t > 0)
    def _():
        pbuf_ref[0:POOL_HALO, :] = pbuf_ref[tile:tile + POOL_HALO, :]

    pbuf_ref[POOL_HALO:POOL_HALO + tile, :] = z_ref[:, 0:POOL_WIDTH]
    pos = t * tile + lax.broadcasted_iota(jnp.int32, (tile, POOL_GROUP), 0)
    for gi, w in enumerate(POOL_WINDOWS):
        cs = slice(gi * POOL_GROUP, (gi + 1) * POOL_GROUP)
        cur = pbuf_ref[POOL_HALO:POOL_HALO + tile, cs]
        span = 1
        off = POOL_HALO - (w - 1)
        s = pbuf_ref[off:POOL_HALO + tile, cs]
        while span < w:
            n = s.shape[0] - span
            s = s[span:span + n] + s[0:n]
            span *= 2
        cnt = jnp.minimum(pos + 1, w).astype(F32)
        pooled = s / cnt - cur
        y = jnp.dot(pooled.astype(BF16), wpool_ref[gi], preferred_element_type=F32) * pscale_ref[:, cs]
        cat_ref[:, cs] = y.astype(BF16)

    zz = _gelu_tanh(z_ref[:, POOL_WIDTH:POOL_WIDTH + 2 * SGU_WIDTH])
    u = zz[:, 0:SGU_WIDTH]
    v = _layer_norm(zz[:, SGU_WIDTH:2 * SGU_WIDTH], lg_ref[...], lb_ref[...]).astype(BF16)
    ti = lax.broadcasted_iota(jnp.int32, (SGU_CHUNK, SGU_CHUNK), 0)
    si = lax.broadcasted_iota(jnp.int32, (SGU_CHUNK, SGU_CHUNK), 1)
    causal = ti >= si
    for gi in range(SGU_HEADS):
        cs = slice(gi * SGU_HEAD_DIM, (gi + 1) * SGU_HEAD_DIM)
        wg = jnp.where(causal, ws_ref[gi], 0.0).astype(BF16)
        for c in range(tile // SGU_CHUNK):
            rs = slice(c * SGU_CHUNK, (c + 1) * SGU_CHUNK)
            mixed = jnp.dot(wg, v[rs, cs], preferred_element_type=F32) + bs_ref[:, cs]
            cat_ref[rs, POOL_WIDTH + gi * SGU_HEAD_DIM:POOL_WIDTH + (gi + 1) * SGU_HEAD_DIM] = (
                u[rs, cs] * mixed).astype(BF16)

    o_ref[0] = x + jnp.dot(cat_ref[...], wout_ref[...], preferred_element_type=F32)


def _mixer1(x, g, w_in, w_pool, pool_scale, ln_g, ln_b, w_s, b_s, w_out):
    B, S, D = x.shape
    tile = SEQ_TILE
    bias = jnp.repeat(b_s.T, SGU_HEAD_DIM, axis=1)
    return pl.pallas_call(
        _mixer1_kernel,
        name="mixer1",
        grid=(B, S // tile),
        in_specs=[
            pl.BlockSpec((1, tile, D), lambda b, t: (b, t, 0)),
            _full((1, D)),
            _full((D, IN1_WIDTH)),
            _full((len(POOL_WINDOWS), POOL_GROUP, POOL_GROUP)),
            _full((1, POOL_WIDTH)),
            _full((1, SGU_WIDTH)),
            _full((1, SGU_WIDTH)),
            _full((SGU_HEADS, SGU_CHUNK, SGU_CHUNK)),
            _full((SGU_CHUNK, SGU_WIDTH)),
            _full((D, D)),
        ],
        out_specs=pl.BlockSpec((1, tile, D), lambda b, t: (b, t, 0)),
        out_shape=jax.ShapeDtypeStruct((B, S, D), F32),
        scratch_shapes=[
            pltpu.VMEM((tile, IN1_WIDTH), F32),
            pltpu.VMEM((POOL_HALO + tile, POOL_WIDTH), F32),
            pltpu.VMEM((tile, D), BF16),
        ],
        compiler_params=pltpu.CompilerParams(
            dimension_semantics=("arbitrary", "arbitrary"), vmem_limit_bytes=VMEM_LIMIT),
    )(x, g.reshape(1, D), w_in.astype(BF16), w_pool.astype(BF16), pool_scale.reshape(1, POOL_WIDTH),
      ln_g.reshape(1, SGU_WIDTH), ln_b.reshape(1, SGU_WIDTH), w_s, bias, w_out.astype(BF16))


def _ffn_kernel(x_ref, g_ref, wg_ref, wu_ref, wd_ref, fg_ref, o_ref, *, final_norm):
    x = x_ref[...]
    hn = _rms(x, g_ref[...]).astype(BF16)
    gate = jnp.dot(hn, wg_ref[...], preferred_element_type=F32)
    up = jnp.dot(hn, wu_ref[...], preferred_element_type=F32)
    act = (gate * _sigmoid(gate) * up).astype(BF16)
    y = x + jnp.dot(act, wd_ref[...], preferred_element_type=F32)
    if final_norm:
        y = _rms(y, fg_ref[...])
    o_ref[...] = y


def _ffn(x, g, w_gate, w_up, w_down, final_g, final_norm):
    B, S, D = x.shape
    M = B * S
    d_ff = w_gate.shape[1]
    tile = FFN_TILE
    out = pl.pallas_call(
        functools.partial(_ffn_kernel, final_norm=final_norm),
        name="ffn_final" if final_norm else "ffn",
        grid=(M // tile,),
        in_specs=[
            pl.BlockSpec((tile, D), lambda i: (i, 0)),
            _full((1, D)),
            _full((D, d_ff)),
            _full((D, d_ff)),
            _full((d_ff, D)),
            _full((1, D)),
        ],
        out_specs=pl.BlockSpec((tile, D), lambda i: (i, 0)),
        out_shape=jax.ShapeDtypeStruct((M, D), F32),
        compiler_params=pltpu.CompilerParams(
            dimension_semantics=("parallel",), vmem_limit_bytes=VMEM_LIMIT),
    )(x.reshape(M, D), g.reshape(1, D), w_gate.astype(BF16), w_up.astype(BF16), w_down.astype(BF16),
      final_g.reshape(1, D))
    return out.reshape(B, S, D)


def kernel(x, mix_norm, a_w_in, a_b_in, a_sinks, a_conv_w, a_conv_b, a_cln_g, a_cln_b, a_w_out, c_w_in, c_w_pool, c_pool_scale, c_sln_g, c_sln_b, c_w_s, c_b_s, c_w_out, ffn_norm, ffn_w_gate, ffn_w_up, ffn_w_down, final_norm):
    h = _mixer0(x, mix_norm[0], a_w_in[0], a_b_in[0], a_sinks[0], a_conv_w[0], a_conv_b[0],
                a_cln_g[0], a_cln_b[0], a_w_out[0])
    h = _ffn(h, ffn_norm[0], ffn_w_gate[0], ffn_w_up[0], ffn_w_down[0], final_norm, False)
    h = _mixer1(h, mix_norm[1], c_w_in[0], c_w_pool[0], c_pool_scale[0], c_sln_g[0], c_sln_b[0],
                c_w_s[0], c_b_s[0], c_w_out[0])
    h = _ffn(h, ffn_norm[1], ffn_w_gate[1], ffn_w_up[1], ffn_w_down[1], final_norm, True)
    return h
```

```python
import functools

import jax
import jax.numpy as jnp
from jax import lax
from jax.experimental import pallas as pl
from jax.experimental.pallas import tpu as pltpu

F32 = jnp.float32
BF16 = jnp.bfloat16

D_MODEL = 1024
N_Q_HEADS = 8
N_KV_HEADS = 2
HEAD_DIM = 64
Q_GROUP = N_Q_HEADS // N_KV_HEADS
ATTN_BLOCK = 128
ATTN_WIDTH = N_Q_HEADS * HEAD_DIM
KV_WIDTH = N_KV_HEADS * HEAD_DIM
CONV_WIDTH = D_MODEL - ATTN_WIDTH
CONV_KERNEL = 31
IN0_WIDTH = ATTN_WIDTH + 2 * KV_WIDTH + 2 * CONV_WIDTH
POOL_WINDOWS = (2, 4, 8, 16)
POOL_WIDTH = 512
POOL_GROUP = POOL_WIDTH // len(POOL_WINDOWS)
SGU_WIDTH = 512
SGU_HEADS = 4
SGU_HEAD_DIM = SGU_WIDTH // SGU_HEADS
SGU_CHUNK = 128
IN1_WIDTH = POOL_WIDTH + 2 * SGU_WIDTH
EPS = 1e-5

LANES = 128
SUBLANES = 8
MXU_COLS = 256
CONV_HALO = 32
CONV_ROWS = 64
POOL_HALO = 16
SEQ_TILE = 512
VMEM_LIMIT = 60 * 1024 * 1024


def _rms(x, g):
    return x * lax.rsqrt(jnp.mean(x * x, axis=-1, keepdims=True) + EPS) * g


def _layer_norm(x, g, b):
    mu = jnp.mean(x, axis=-1, keepdims=True)
    xc = x - mu
    var = jnp.mean(xc * xc, axis=-1, keepdims=True)
    return xc * lax.rsqrt(var + EPS) * g + b


def _sigmoid(x):
    return 1.0 / (1.0 + jnp.exp(-x))


def _gelu_tanh(x):
    c = 0.7978845608028654
    return 0.5 * x * (1.0 + jnp.tanh(c * (x + 0.044715 * (x * x * x))))


def _emit_interleaved(streams):
    totals = [sum(c for c, _ in s) for s in streams]
    pos = [0] * len(streams)
    done = [0.0] * len(streams)
    while True:
        live = [k for k in range(len(streams)) if pos[k] < len(streams[k])]
        if not live:
            return
        k = min(live, key=lambda j: done[j] / totals[j])
        cost, thunk = streams[k][pos[k]]
        thunk()
        pos[k] += 1
        done[k] += cost


def _back_stream(xp_ref, cat_ref, slot, wout_ref, fg_ref, wg_ref, wu_ref, wd_ref, ng_ref, o_ref,
                 h_ref, hb_ref, act_ref, final_norm):
    units = []
    d = D_MODEL
    d_ff = wg_ref.shape[1]

    def out_proj(c):
        cs = slice(c * MXU_COLS, (c + 1) * MXU_COLS)
        h_ref[:, cs] = xp_ref[:, cs] + jnp.dot(cat_ref[slot], wout_ref[:, cs], preferred_element_type=F32)

    for c in range(d // MXU_COLS):
        units.append((512, functools.partial(out_proj, c)))

    def norm():
        hb_ref[...] = _rms(h_ref[...], fg_ref[...]).astype(BF16)

    units.append((500, norm))

    def gate_up(c):
        cs = slice(c * MXU_COLS, (c + 1) * MXU_COLS)
        hb = hb_ref[...]
        gate = jnp.dot(hb, wg_ref[:, cs], preferred_element_type=F32)
        up = jnp.dot(hb, wu_ref[:, cs], preferred_element_type=F32)
        act_ref[:, cs] = (gate * _sigmoid(gate) * up).astype(BF16)

    for c in range(d_ff // MXU_COLS):
        units.append((1024, functools.partial(gate_up, c)))

    def down(c):
        cs = slice(c * MXU_COLS, (c + 1) * MXU_COLS)
        y = h_ref[:, cs] + jnp.dot(act_ref[...], wd_ref[:, cs], preferred_element_type=F32)
        if final_norm:
            h_ref[:, cs] = y
        else:
            o_ref[:, cs] = y

    for c in range(d // MXU_COLS):
        units.append((1408, functools.partial(down, c)))

    if final_norm:
        def last():
            o_ref[...] = _rms(h_ref[...], ng_ref[...])

        units.append((500, last))
    return units


def _layer0_kernel(x_ref, xp_ref, g_ref, win_ref, bin_ref, sink_ref, cw_ref, cb_ref, lg_ref, lb_ref,
                   wout_ref, fg_ref, wg_ref, wu_ref, wd_ref, ng_ref, o_ref,
                   hn_ref, z_ref, kd_ref, vd_ref, hbuf_ref, conv_ref, cat_ref, h_ref, hb_ref, act_ref,
                   *, tiles_per_seq):
    i = pl.program_id(0)
    t = i % tiles_per_seq
    slot = i % 2
    tile = x_ref.shape[0]
    blk = ATTN_BLOCK

    @pl.when(i == 0)
    def _():
        cat_ref[...] = jnp.zeros(cat_ref.shape, BF16)

    @pl.when(t == 0)
    def _():
        kd_ref[:, 0:blk, :] = jnp.zeros((N_KV_HEADS, blk, LANES), BF16)
        vd_ref[:, 0:blk, :] = jnp.zeros((N_KV_HEADS, blk, LANES), BF16)
        hbuf_ref[0:CONV_HALO, :] = jnp.zeros((CONV_HALO, CONV_WIDTH), F32)

    @pl.when(t > 0)
    def _():
        kd_ref[:, 0:blk, :] = kd_ref[:, tile:tile + blk, :]
        vd_ref[:, 0:blk, :] = vd_ref[:, tile:tile + blk, :]
        hbuf_ref[0:CONV_HALO, :] = hbuf_ref[tile:tile + CONV_HALO, :]

    front = []

    def norm():
        hn_ref[...] = _rms(x_ref[...], g_ref[...]).astype(BF16)

    front.append((500, norm))

    def in_proj(c):
        cs = slice(c * MXU_COLS, (c + 1) * MXU_COLS)
        z_ref[:, cs] = jnp.dot(hn_ref[...], win_ref[:, cs], preferred_element_type=F32) + bin_ref[:, cs]

    for c in range(IN0_WIDTH // MXU_COLS):
        front.append((512, functools.partial(in_proj, c)))

    def kv_dup():
        lo_t = lax.broadcasted_iota(jnp.int32, (tile, LANES), 1) < HEAD_DIM
        zk = z_ref[:, ATTN_WIDTH:ATTN_WIDTH + KV_WIDTH]
        zv = z_ref[:, ATTN_WIDTH + KV_WIDTH:ATTN_WIDTH + 2 * KV_WIDTH]
        zk_sw = pltpu.roll(zk, HEAD_DIM, 1)
        zv_sw = pltpu.roll(zv, HEAD_DIM, 1)
        kd_ref[0, blk:blk + tile, :] = jnp.where(lo_t, zk, zk_sw).astype(BF16)
        kd_ref[1, blk:blk + tile, :] = jnp.where(lo_t, zk_sw, zk).astype(BF16)
        vd_ref[0, blk:blk + tile, :] = jnp.where(lo_t, zv, zv_sw).astype(BF16)
        vd_ref[1, blk:blk + tile, :] = jnp.where(lo_t, zv_sw, zv).astype(BF16)

    front.append((150, kv_dup))

    def glu(c):
        rs = slice(c * CONV_ROWS, (c + 1) * CONV_ROWS)
        c0 = ATTN_WIDTH + 2 * KV_WIDTH
        a = z_ref[rs, c0:c0 + CONV_WIDTH]
        gate = z_ref[rs, c0 + CONV_WIDTH:c0 + 2 * CONV_WIDTH]
        hbuf_ref[CONV_HALO + c * CONV_ROWS:CONV_HALO + (c + 1) * CONV_ROWS, :] = a * _sigmoid(gate)

    for c in range(tile // CONV_ROWS):
        front.append((60, functools.partial(glu, c)))

    rows = Q_GROUP * blk
    scale = HEAD_DIM ** -0.5

    def masks():
        qi = lax.broadcasted_iota(jnp.int32, (rows, blk), 0) % blk
        col = lax.broadcasted_iota(jnp.int32, (rows, blk), 1)
        from_prev = col > qi
        lo = lax.broadcasted_iota(jnp.int32, (blk, LANES), 1) < HEAD_DIM
        return from_prev, lo

    def attn_block(b, kh):
        from_prev, lo = masks()
        rs = slice(b * blk, (b + 1) * blk)
        parts = []
        for j in range(Q_GROUP // 2):
            c0 = kh * Q_GROUP * HEAD_DIM + j * LANES
            qp = z_ref[rs, c0:c0 + LANES] * scale
            parts.append(jnp.where(lo, qp, 0.0))
            parts.append(jnp.where(lo, 0.0, qp))
        qs = jnp.concatenate(parts, axis=0).astype(BF16)
        kw = kd_ref[kh, b * blk:(b + 2) * blk, :]
        both = lax.dot_general(qs, kw, (((1,), (1,)), ((), ())), preferred_element_type=F32)
        logits = jnp.where(from_prev, both[:, 0:blk], both[:, blk:2 * blk])
        if b == 0:
            logits = jnp.where(from_prev & (t == 0), -jnp.inf, logits)
        sink = sink_ref[kh]
        m = jnp.maximum(jnp.max(logits, axis=-1, keepdims=True), sink)
        p = jnp.exp(logits - m)
        den = jnp.sum(p, axis=-1, keepdims=True) + jnp.exp(sink - m)
        pw = jnp.concatenate([jnp.where(from_prev, p, 0.0), jnp.where(from_prev, 0.0, p)], axis=1)
        vw = vd_ref[kh, b * blk:(b + 2) * blk, :]
        o = jnp.dot(pw.astype(BF16), vw, preferred_element_type=F32) / den
        for j in range(Q_GROUP // 2):
            pair = jnp.where(lo, o[2 * j * blk:(2 * j + 1) * blk], o[(2 * j + 1) * blk:(2 * j + 2) * blk])
            c0 = kh * Q_GROUP * HEAD_DIM + j * LANES
            cat_ref[slot, rs, c0:c0 + LANES] = pair.astype(BF16)

    def conv_slab(c, cg):
        rs = slice(c * CONV_ROWS, (c + 1) * CONV_ROWS)
        ls = slice(cg * LANES, (cg + 1) * LANES)
        acc = None
        for r in range(SUBLANES):
            pad = SUBLANES if r else 0
            n = CONV_ROWS + pad
            row0 = CONV_HALO + c * CONV_ROWS - pad
            part = None
            for a in range((CONV_KERNEL - 1 - r) // SUBLANES + 1):
                k = CONV_KERNEL - 1 - (SUBLANES * a + r)
                src = row0 - SUBLANES * a
                term = cw_ref[k:k + 1, ls] * hbuf_ref[src:src + n, ls]
                part = term if part is None else part + term
            if r:
                part = part[SUBLANES - r:SUBLANES - r + CONV_ROWS]
            acc = part if acc is None else acc + part
        conv_ref[rs, ls] = acc + cb_ref[:, ls]

    def conv_norm(c):
        rs = slice(c * CONV_ROWS, (c + 1) * CONV_ROWS)
        y = _layer_norm(conv_ref[rs, :], lg_ref[...], lb_ref[...])
        y = y * _sigmoid(y)
        cat_ref[slot, rs, ATTN_WIDTH:ATTN_WIDTH + CONV_WIDTH] = y.astype(BF16)

    steps = [(b, kh) for b in range(tile // blk) for kh in range(N_KV_HEADS)]
    slabs = [(c, cg) for c in range(tile // CONV_ROWS) for cg in range(CONV_WIDTH // LANES)]
    slabs_per_step = len(slabs) // len(steps)
    for n, (b, kh) in enumerate(steps):
        front.append((700, functools.partial(attn_block, b, kh)))
        for c, cg in slabs[n * slabs_per_step:(n + 1) * slabs_per_step]:
            front.append((170, functools.partial(conv_slab, c, cg)))
            if cg == CONV_WIDTH // LANES - 1:
                front.append((150, functools.partial(conv_norm, c)))

    back = _back_stream(xp_ref, cat_ref, 1 - slot, wout_ref, fg_ref, wg_ref, wu_ref, wd_ref, ng_ref, o_ref,
                        h_ref, hb_ref, act_ref, final_norm=False)
    _emit_interleaved([front, back])


def _full(shape):
    return pl.BlockSpec(shape, lambda *_: (0,) * len(shape))


def _layer_call(body, name, x, front_args, front_specs, front_scratch, w_out, ffn_g, w_gate, w_up, w_down,
                final_g):
    B, S, D = x.shape
    M = B * S
    tile = SEQ_TILE
    n_tiles = M // tile
    d_ff = w_gate.shape[1]
    row = lambda v: v.reshape(1, -1)
    out = pl.pallas_call(
        functools.partial(body, tiles_per_seq=S // tile),
        name=name,
        grid=(n_tiles + 1,),
        in_specs=[
            pl.BlockSpec((tile, D), lambda i: (jnp.minimum(i, n_tiles - 1), 0)),
            pl.BlockSpec((tile, D), lambda i: (jnp.maximum(i - 1, 0), 0)),
            *front_specs,
            _full((D, D)),
            _full((1, D)),
            _full((D, d_ff)),
            _full((D, d_ff)),
            _full((d_ff, D)),
            _full((1, D)),
        ],
        out_specs=pl.BlockSpec((tile, D), lambda i: (jnp.maximum(i - 1, 0), 0)),
        out_shape=jax.ShapeDtypeStruct((M, D), F32),
        scratch_shapes=[
            pltpu.VMEM((tile, D), BF16),
            *front_scratch,
            pltpu.VMEM((2, tile, D), BF16),
            pltpu.VMEM((tile, D), F32),
            pltpu.VMEM((tile, D), BF16),
            pltpu.VMEM((tile, d_ff), BF16),
        ],
        compiler_params=pltpu.CompilerParams(
            dimension_semantics=("arbitrary",), vmem_limit_bytes=VMEM_LIMIT),
    )(x.reshape(M, D), x.reshape(M, D), *front_args, w_out.astype(BF16), row(ffn_g), w_gate.astype(BF16),
      w_up.astype(BF16), w_down.astype(BF16), row(final_g))
    return out.reshape(B, S, D)


def _layer0(x, g, w_in, b_in, sinks, conv_w, conv_b, ln_g, ln_b, w_out, ffn_g, w_gate, w_up, w_down, final_g):
    tile = SEQ_TILE
    rows = Q_GROUP * ATTN_BLOCK
    row = lambda v: v.reshape(1, -1)
    sink_cols = jnp.repeat(sinks.reshape(N_KV_HEADS, Q_GROUP), ATTN_BLOCK, axis=1).reshape(N_KV_HEADS, rows, 1)
    front_args = (row(g), w_in.astype(BF16), row(b_in), sink_cols, conv_w, row(conv_b), row(ln_g), row(ln_b))
    front_specs = [
        _full((1, D_MODEL)),
        _full((D_MODEL, IN0_WIDTH)),
        _full((1, IN0_WIDTH)),
        _full((N_KV_HEADS, rows, 1)),
        _full((CONV_KERNEL, CONV_WIDTH)),
        _full((1, CONV_WIDTH)),
        _full((1, CONV_WIDTH)),
        _full((1, CONV_WIDTH)),
    ]
    front_scratch = [
        pltpu.VMEM((tile, IN0_WIDTH), F32),
        pltpu.VMEM((N_KV_HEADS, ATTN_BLOCK + tile, LANES), BF16),
        pltpu.VMEM((N_KV_HEADS, ATTN_BLOCK + tile, LANES), BF16),
        pltpu.VMEM((CONV_HALO + tile, CONV_WIDTH), F32),
        pltpu.VMEM((tile, CONV_WIDTH), F32),
    ]
    return _layer_call(_layer0_kernel, "layer0", x, front_args, front_specs, front_scratch,
                       w_out, ffn_g, w_gate, w_up, w_down, final_g)


def _layer1_kernel(x_ref, xp_ref, g_ref, win_ref, wpool_ref, pscale_ref, lg_ref, lb_ref, ws_ref, bs_ref,
                   wout_ref, fg_ref, wg_ref, wu_ref, wd_ref, ng_ref, o_ref,
                   hn_ref, z_ref, pbuf_ref, u_ref, v_ref, wsm_ref, cat_ref, h_ref, hb_ref, act_ref,
                   *, tiles_per_seq):
    i = pl.program_id(0)
    t = i % tiles_per_seq
    slot = i % 2
    tile = x_ref.shape[0]

    @pl.when(i == 0)
    def _():
        cat_ref[...] = jnp.zeros(cat_ref.shape, BF16)

    @pl.when(t == 0)
    def _():
        pbuf_ref[0:POOL_HALO, :] = jnp.zeros((POOL_HALO, POOL_WIDTH), F32)

    @pl.when(t > 0)
    def _():
        pbuf_ref[0:POOL_HALO, :] = pbuf_ref[tile:tile + POOL_HALO, :]

    front = []

    def norm():
        hn_ref[...] = _rms(x_ref[...], g_ref[...]).astype(BF16)
        ti = lax.broadcasted_iota(jnp.int32, (SGU_CHUNK, SGU_CHUNK), 0)
        si = lax.broadcasted_iota(jnp.int32, (SGU_CHUNK, SGU_CHUNK), 1)
        for gi in range(SGU_HEADS):
            wsm_ref[gi] = jnp.where(ti >= si, ws_ref[gi], 0.0).astype(BF16)

    front.append((550, norm))

    def in_proj(c):
        cs = slice(c * MXU_COLS, (c + 1) * MXU_COLS)
        z_ref[:, cs] = jnp.dot(hn_ref[...], win_ref[:, cs], preferred_element_type=F32)

    for c in range(IN1_WIDTH // MXU_COLS):
        front.append((512, functools.partial(in_proj, c)))

    def pool_fill():
        pbuf_ref[POOL_HALO:POOL_HALO + tile, :] = z_ref[:, 0:POOL_WIDTH]

    front.append((70, pool_fill))

    def pool(gi, w):
        cs = slice(gi * POOL_GROUP, (gi + 1) * POOL_GROUP)
        pos = t * tile + lax.broadcasted_iota(jnp.int32, (tile, POOL_GROUP), 0)
        cur = pbuf_ref[POOL_HALO:POOL_HALO + tile, cs]
        s = pbuf_ref[POOL_HALO - (w - 1):POOL_HALO + tile, cs]
        span = 1
        while span < w:
            n = s.shape[0] - span
            s = s[span:span + n] + s[0:n]
            span *= 2
        cnt = jnp.minimum(pos + 1, w).astype(F32)
        pooled = s / cnt - cur
        y = jnp.dot(pooled.astype(BF16), wpool_ref[gi], preferred_element_type=F32) * pscale_ref[:, cs]
        cat_ref[slot, :, cs] = y.astype(BF16)

    for gi, w in enumerate(POOL_WINDOWS):
        front.append((250, functools.partial(pool, gi, w)))

    def gate_prep(c):
        rs = slice(c * SGU_CHUNK, (c + 1) * SGU_CHUNK)
        zz = _gelu_tanh(z_ref[rs, POOL_WIDTH:POOL_WIDTH + 2 * SGU_WIDTH])
        u_ref[rs, :] = zz[:, 0:SGU_WIDTH]
        v_ref[rs, :] = _layer_norm(zz[:, SGU_WIDTH:2 * SGU_WIDTH], lg_ref[...], lb_ref[...]).astype(BF16)

    def gate_mix(c, gi):
        rs = slice(c * SGU_CHUNK, (c + 1) * SGU_CHUNK)
        cs = slice(gi * SGU_HEAD_DIM, (gi + 1) * SGU_HEAD_DIM)
        mixed = jnp.dot(wsm_ref[gi], v_ref[rs, cs], preferred_element_type=F32) + bs_ref[:, cs]
        cat_ref[slot, rs, POOL_WIDTH + gi * SGU_HEAD_DIM:POOL_WIDTH + (gi + 1) * SGU_HEAD_DIM] = (
            u_ref[rs, cs] * mixed).astype(BF16)

    for c in range(tile // SGU_CHUNK):
        front.append((450, functools.partial(gate_prep, c)))
        for gi in range(SGU_HEADS):
            front.append((130, functools.partial(gate_mix, c, gi)))

    back = _back_stream(xp_ref, cat_ref, 1 - slot, wout_ref, fg_ref, wg_ref, wu_ref, wd_ref, ng_ref, o_ref,
                        h_ref, hb_ref, act_ref, final_norm=True)
    _emit_interleaved([front, back])


def _layer1(x, g, w_in, w_pool, pool_scale, ln_g, ln_b, w_s, b_s, w_out, ffn_g, w_gate, w_up, w_down, final_g):
    tile = SEQ_TILE
    row = lambda v: v.reshape(1, -1)
    bias = jnp.repeat(b_s.T, SGU_HEAD_DIM, axis=1)
    front_args = (row(g), w_in.astype(BF16), w_pool.astype(BF16), row(pool_scale), row(ln_g), row(ln_b), w_s, bias)
    front_specs = [
        _full((1, D_MODEL)),
        _full((D_MODEL, IN1_WIDTH)),
        _full((len(POOL_WINDOWS), POOL_GROUP, POOL_GROUP)),
        _full((1, POOL_WIDTH)),
        _full((1, SGU_WIDTH)),
        _full((1, SGU_WIDTH)),
        _full((SGU_HEADS, SGU_CHUNK, SGU_CHUNK)),
        _full((SGU_CHUNK, SGU_WIDTH)),
    ]
    front_scratch = [
        pltpu.VMEM((tile, IN1_WIDTH), F32),
        pltpu.VMEM((POOL_HALO + tile, POOL_WIDTH), F32),
        pltpu.VMEM((tile, SGU_WIDTH), F32),
        pltpu.VMEM((tile, SGU_WIDTH), BF16),
        pltpu.VMEM((SGU_HEADS, SGU_CHUNK, SGU_CHUNK), BF16),
    ]
    return _layer_call(_layer1_kernel, "layer1", x, front_args, front_specs, front_scratch,
                       w_out, ffn_g, w_gate, w_up, w_down, final_g)


def kernel(x, mix_norm, a_w_in, a_b_in, a_sinks, a_conv_w, a_conv_b, a_cln_g, a_cln_b, a_w_out, c_w_in, c_w_pool, c_pool_scale, c_sln_g, c_sln_b, c_w_s, c_b_s, c_w_out, ffn_norm, ffn_w_gate, ffn_w_up, ffn_w_down, final_norm):
    h = _layer0(x, mix_norm[0], a_w_in[0], a_b_in[0], a_sinks[0], a_conv_w[0], a_conv_b[0], a_cln_g[0],
                a_cln_b[0], a_w_out[0], ffn_norm[0], ffn_w_gate[0], ffn_w_up[0], ffn_w_down[0], final_norm)
    h = _layer1(h, mix_norm[1], c_w_in[0], c_w_pool[0], c_pool_scale[0], c_sln_g[0], c_sln_b[0], c_w_s[0],
                c_b_s[0], c_w_out[0], ffn_norm[1], ffn_w_gate[1], ffn_w_up[1], ffn_w_down[1], final_norm)
    return h
```

```python
import functools

import jax
import jax.numpy as jnp
from jax import lax
from jax.experimental import pallas as pl
from jax.experimental.pallas import tpu as pltpu

F32 = jnp.float32
BF16 = jnp.bfloat16

D_MODEL = 1024
N_Q_HEADS = 8
N_KV_HEADS = 2
HEAD_DIM = 64
Q_GROUP = N_Q_HEADS // N_KV_HEADS
ATTN_BLOCK = 128
ATTN_WIDTH = N_Q_HEADS * HEAD_DIM
KV_WIDTH = N_KV_HEADS * HEAD_DIM
CONV_WIDTH = D_MODEL - ATTN_WIDTH
CONV_KERNEL = 31
IN0_WIDTH = ATTN_WIDTH + 2 * KV_WIDTH + 2 * CONV_WIDTH
POOL_WINDOWS = (2, 4, 8, 16)
POOL_WIDTH = 512
POOL_GROUP = POOL_WIDTH // len(POOL_WINDOWS)
SGU_WIDTH = 512
SGU_HEADS = 4
SGU_HEAD_DIM = SGU_WIDTH // SGU_HEADS
SGU_CHUNK = 128
IN1_WIDTH = POOL_WIDTH + 2 * SGU_WIDTH
EPS = 1e-5

LANES = 128
SUBLANES = 8
MXU_COLS = 256
CONV_HALO = 32
CONV_ROWS = 64
POOL_HALO = 16
SEQ_TILE = 512
VMEM_LIMIT = 60 * 1024 * 1024


def _rms(x, g):
    return x * lax.rsqrt(jnp.mean(x * x, axis=-1, keepdims=True) + EPS) * g


def _layer_norm(x, g, b):
    mu = jnp.mean(x, axis=-1, keepdims=True)
    xc = x - mu
    var = jnp.mean(xc * xc, axis=-1, keepdims=True)
    return xc * lax.rsqrt(var + EPS) * g + b


def _sigmoid(x):
    return 1.0 / (1.0 + jnp.exp(-x))


def _gelu_tanh(x):
    c = 0.7978845608028654
    return 0.5 * x * (1.0 + jnp.tanh(c * (x + 0.044715 * (x * x * x))))


class _Ties:
    def __init__(self):
        self.pieces = []

    def add(self, value):
        for j in range(value.shape[0] // SUBLANES):
            self.pieces.append(pltpu.bitcast(value[j * SUBLANES:(j + 1) * SUBLANES], jnp.uint32))

    def bind(self, result):
        if not self.pieces:
            return result
        tile_rows = 2 * SUBLANES if result.dtype == BF16 else SUBLANES
        n_rt, n_lt = result.shape[0] // tile_rows, result.shape[1] // LANES
        zeros = []
        for piece in self.pieces:
            zero = lax.shift_right_logical(lax.shift_right_logical(piece, jnp.uint32(16)), jnp.uint32(16))
            zeros.append(pltpu.bitcast(zero, result.dtype))
        grid = []
        for rt in range(n_rt):
            row = [zeros[(rt * n_lt + lt) * len(zeros) // (n_rt * n_lt)] for lt in range(n_lt)]
            grid.append(jnp.concatenate(row, axis=1))
        self.pieces = []
        return result + jnp.concatenate(grid, axis=0)


def _emit_interleaved(front, back, front_end):
    streams = [front, back]
    pace = [front_end, 1.0]
    totals = [sum(c for c, _ in s) for s in streams]
    pos = [0] * len(streams)
    done = [0.0] * len(streams)
    while True:
        live = [k for k in range(len(streams)) if pos[k] < len(streams[k])]
        if not live:
            return
        k = min(live, key=lambda j: done[j] / totals[j] * pace[j])
        cost, thunk = streams[k][pos[k]]
        thunk()
        pos[k] += 1
        done[k] += cost


def _back_stream(ties, xp_ref, cat_view, wout_ref, fg_ref, wg_ref, wu_ref, wd_ref, ng_ref, o_ref,
                 h_ref, hb_ref, act_ref, final_norm):
    units = []
    d = D_MODEL
    d_ff = wg_ref.shape[1]

    def out_proj(c):
        cs = slice(c * MXU_COLS, (c + 1) * MXU_COLS)
        h_ref[:, cs] = xp_ref[:, cs] + jnp.dot(cat_view[...], wout_ref[:, cs], preferred_element_type=F32)

    for c in range(d // MXU_COLS):
        units.append((512, functools.partial(out_proj, c)))

    def norm():
        hb_ref[...] = _rms(h_ref[...], fg_ref[...]).astype(BF16)

    units.append((500, norm))

    def gate_up(c):
        cs = slice(c * MXU_COLS, (c + 1) * MXU_COLS)
        hb = hb_ref[...]
        gate = jnp.dot(hb, wg_ref[:, cs], preferred_element_type=F32)
        up = jnp.dot(hb, wu_ref[:, cs], preferred_element_type=F32)
        act_ref[:, cs] = ties.bind((gate * _sigmoid(gate) * up).astype(BF16))

    for c in range(d_ff // MXU_COLS):
        units.append((1024, functools.partial(gate_up, c)))

    def down(c):
        cs = slice(c * MXU_COLS, (c + 1) * MXU_COLS)
        y = ties.bind(h_ref[:, cs] + jnp.dot(act_ref[...], wd_ref[:, cs], preferred_element_type=F32))
        if final_norm:
            h_ref[:, cs] = y
        else:
            o_ref[:, cs] = y

    for c in range(d // MXU_COLS):
        units.append((1408, functools.partial(down, c)))

    if final_norm:
        def last():
            o_ref[...] = _rms(h_ref[...], ng_ref[...])

        units.append((500, last))
    return units


def _layer0_kernel(x_ref, xp_ref, g_ref, win_ref, bin_ref, sink_ref, cw_ref, cb_ref, lg_ref, lb_ref,
                   wout_ref, fg_ref, wg_ref, wu_ref, wd_ref, ng_ref, o_ref,
                   hn_ref, z_ref, kd_ref, vd_ref, hbuf_ref, conv_ref, cat_ref, h_ref, hb_ref, act_ref,
                   *, tiles_per_seq):
    i = pl.program_id(0)
    new = i % 2
    old = 1 - new
    t_new = i % tiles_per_seq
    tile = x_ref.shape[0]
    blk = ATTN_BLOCK
    ties = _Ties()

    @pl.when(i == 0)
    def _():
        cat_ref[...] = jnp.zeros(cat_ref.shape, BF16)

    @pl.when(t_new == 0)
    def _():
        kd_ref[:, 0:blk, :] = jnp.zeros((N_KV_HEADS, blk, LANES), BF16)
        vd_ref[:, 0:blk, :] = jnp.zeros((N_KV_HEADS, blk, LANES), BF16)

    @pl.when(t_new > 0)
    def _():
        kd_ref[:, 0:blk, :] = kd_ref[:, tile:tile + blk, :]
        vd_ref[:, 0:blk, :] = vd_ref[:, tile:tile + blk, :]

    @pl.when(t_new == 0)
    def _():
        hbuf_ref[0:CONV_HALO, :] = jnp.zeros((CONV_HALO, CONV_WIDTH), F32)

    @pl.when(t_new > 0)
    def _():
        hbuf_ref[0:CONV_HALO, :] = hbuf_ref[tile:tile + CONV_HALO, :]

    front = []

    def norm():
        hn_ref[...] = _rms(x_ref[...], g_ref[...]).astype(BF16)

    front.append((500, norm))

    def in_proj(c):
        cs = slice(c * MXU_COLS, (c + 1) * MXU_COLS)
        z_ref[:, cs] = jnp.dot(hn_ref[...], win_ref[:, cs], preferred_element_type=F32) + bin_ref[:, cs]

    for c in range(IN0_WIDTH // MXU_COLS):
        front.append((512, functools.partial(in_proj, c)))

    def kv_dup():
        lo_t = lax.broadcasted_iota(jnp.int32, (tile, LANES), 1) < HEAD_DIM
        zk = z_ref[:, ATTN_WIDTH:ATTN_WIDTH + KV_WIDTH]
        zv = z_ref[:, ATTN_WIDTH + KV_WIDTH:ATTN_WIDTH + 2 * KV_WIDTH]
        zk_sw = pltpu.roll(zk, HEAD_DIM, 1)
        zv_sw = pltpu.roll(zv, HEAD_DIM, 1)
        kd_ref[0, blk:blk + tile, :] = jnp.where(lo_t, zk, zk_sw).astype(BF16)
        kd_ref[1, blk:blk + tile, :] = jnp.where(lo_t, zk_sw, zk).astype(BF16)
        vd_ref[0, blk:blk + tile, :] = jnp.where(lo_t, zv, zv_sw).astype(BF16)
        vd_ref[1, blk:blk + tile, :] = jnp.where(lo_t, zv_sw, zv).astype(BF16)

    front.append((150, kv_dup))

    def glu(c):
        rs = slice(c * CONV_ROWS, (c + 1) * CONV_ROWS)
        c0 = ATTN_WIDTH + 2 * KV_WIDTH
        a = z_ref[rs, c0:c0 + CONV_WIDTH]
        gate = z_ref[rs, c0 + CONV_WIDTH:c0 + 2 * CONV_WIDTH]
        hbuf_ref[CONV_HALO + c * CONV_ROWS:CONV_HALO + (c + 1) * CONV_ROWS, :] = a * _sigmoid(gate)

    for c in range(tile // CONV_ROWS):
        front.append((60, functools.partial(glu, c)))

    rows = Q_GROUP * blk
    scale = HEAD_DIM ** -0.5

    def attn_block(b, kh):
        qi = lax.broadcasted_iota(jnp.int32, (rows, blk), 0) % blk
        col = lax.broadcasted_iota(jnp.int32, (rows, blk), 1)
        from_prev = col > qi
        lo = lax.broadcasted_iota(jnp.int32, (blk, LANES), 1) < HEAD_DIM
        rs = slice(b * blk, (b + 1) * blk)
        parts = []
        for j in range(Q_GROUP // 2):
            c0 = kh * Q_GROUP * HEAD_DIM + j * LANES
            qp = z_ref[rs, c0:c0 + LANES] * scale
            parts.append(jnp.where(lo, qp, 0.0))
            parts.append(jnp.where(lo, 0.0, qp))
        qs = jnp.concatenate(parts, axis=0).astype(BF16)
        kw = kd_ref[kh, b * blk:(b + 2) * blk, :]
        both = lax.dot_general(qs, kw, (((1,), (1,)), ((), ())), preferred_element_type=F32)
        logits = jnp.where(from_prev, both[:, 0:blk], both[:, blk:2 * blk])
        if b == 0:
            logits = jnp.where(from_prev & (t_new == 0), -jnp.inf, logits)
        sink = sink_ref[kh]
        m = jnp.maximum(jnp.max(logits, axis=-1, keepdims=True), sink)
        p = jnp.exp(logits - m)
        den = jnp.sum(p, axis=-1, keepdims=True) + jnp.exp(sink - m)
        pw = jnp.concatenate([jnp.where(from_prev, p, 0.0), jnp.where(from_prev, 0.0, p)], axis=1)
        vw = vd_ref[kh, b * blk:(b + 2) * blk, :]
        o = jnp.dot(pw.astype(BF16), vw, preferred_element_type=F32) / den
        for j in range(Q_GROUP // 2):
            pair = jnp.where(lo, o[2 * j * blk:(2 * j + 1) * blk], o[(2 * j + 1) * blk:(2 * j + 2) * blk])
            c0 = kh * Q_GROUP * HEAD_DIM + j * LANES
            cat_ref[new, rs, c0:c0 + LANES] = pair.astype(BF16)
        ties.add(o)

    def conv_slab(c, cg):
        rs = slice(c * CONV_ROWS, (c + 1) * CONV_ROWS)
        ls = slice(cg * LANES, (cg + 1) * LANES)
        depth = (CONV_KERNEL - 1) // SUBLANES
        top = CONV_HALO + c * CONV_ROWS - (depth + 1) * SUBLANES
        win = hbuf_ref[top:top + (depth + 1) * SUBLANES + CONV_ROWS, ls]
        acc = None
        for r in range(SUBLANES):
            pad = SUBLANES if r else 0
            n = CONV_ROWS + pad
            part = None
            for a in range((CONV_KERNEL - 1 - r) // SUBLANES + 1):
                k = CONV_KERNEL - 1 - (SUBLANES * a + r)
                src = (depth + 1 - a) * SUBLANES - pad
                term = cw_ref[k:k + 1, ls] * win[src:src + n]
                part = term if part is None else part + term
            if r:
                part = part[SUBLANES - r:SUBLANES - r + CONV_ROWS]
            acc = part if acc is None else acc + part
        conv_ref[rs, ls] = acc + cb_ref[:, ls]
        ties.add(acc)

    def conv_norm(c):
        rs = slice(c * CONV_ROWS, (c + 1) * CONV_ROWS)
        y = _layer_norm(conv_ref[rs, :], lg_ref[...], lb_ref[...])
        y = y * _sigmoid(y)
        cat_ref[new, rs, ATTN_WIDTH:ATTN_WIDTH + CONV_WIDTH] = y.astype(BF16)
        for cg in range(CONV_WIDTH // LANES):
            ties.add(y[:, cg * LANES:(cg + 1) * LANES])

    steps = [(b, kh) for b in range(tile // blk) for kh in range(N_KV_HEADS)]
    slabs = [(c, cg) for c in range(tile // CONV_ROWS) for cg in range(CONV_WIDTH // LANES)]
    slabs_per_step = len(slabs) // len(steps)
    for n, (b, kh) in enumerate(steps):
        front.append((700, functools.partial(attn_block, b, kh)))
        for c, cg in slabs[n * slabs_per_step:(n + 1) * slabs_per_step]:
            front.append((170, functools.partial(conv_slab, c, cg)))
            if cg == CONV_WIDTH // LANES - 1:
                front.append((150, functools.partial(conv_norm, c)))

    back = _back_stream(ties, xp_ref, cat_ref.at[old], wout_ref, fg_ref, wg_ref, wu_ref, wd_ref, ng_ref, o_ref,
                        h_ref, hb_ref, act_ref, final_norm=False)
    _emit_interleaved(front, back, front_end=0.75)


def _full(shape):
    return pl.BlockSpec(shape, lambda *_: (0,) * len(shape))


def _layer_call(body, name, x, front_args, front_specs, front_scratch, w_out, ffn_g, w_gate, w_up, w_down,
                final_g):
    B, S, D = x.shape
    M = B * S
    tile = SEQ_TILE
    n_tiles = M // tile
    d_ff = w_gate.shape[1]
    row = lambda v: v.reshape(1, -1)
    out = pl.pallas_call(
        functools.partial(body, tiles_per_seq=S // tile),
        name=name,
        grid=(n_tiles + 1,),
        in_specs=[
            pl.BlockSpec((tile, D), lambda i: (jnp.minimum(i, n_tiles - 1), 0)),
            pl.BlockSpec((tile, D), lambda i: (jnp.maximum(i - 1, 0), 0)),
            *front_specs,
            _full((D, D)),
            _full((1, D)),
            _full((D, d_ff)),
            _full((D, d_ff)),
            _full((d_ff, D)),
            _full((1, D)),
        ],
        out_specs=pl.BlockSpec((tile, D), lambda i: (jnp.maximum(i - 1, 0), 0)),
        out_shape=jax.ShapeDtypeStruct((M, D), F32),
        scratch_shapes=[
            pltpu.VMEM((tile, D), BF16),
            *front_scratch,
            pltpu.VMEM((2, tile, D), BF16),
            pltpu.VMEM((tile, D), F32),
            pltpu.VMEM((tile, D), BF16),
            pltpu.VMEM((tile, d_ff), BF16),
        ],
        compiler_params=pltpu.CompilerParams(
            dimension_semantics=("arbitrary",), vmem_limit_bytes=VMEM_LIMIT),
    )(x.reshape(M, D), x.reshape(M, D), *front_args, w_out.astype(BF16), row(ffn_g), w_gate.astype(BF16),
      w_up.astype(BF16), w_down.astype(BF16), row(final_g))
    return out.reshape(B, S, D)


def _layer0(x, g, w_in, b_in, sinks, conv_w, conv_b, ln_g, ln_b, w_out, ffn_g, w_gate, w_up, w_down, final_g):
    tile = SEQ_TILE
    rows = Q_GROUP * ATTN_BLOCK
    row = lambda v: v.reshape(1, -1)
    sink_cols = jnp.repeat(sinks.reshape(N_KV_HEADS, Q_GROUP), ATTN_BLOCK, axis=1).reshape(N_KV_HEADS, rows, 1)
    front_args = (row(g), w_in.astype(BF16), row(b_in), sink_cols, conv_w, row(conv_b), row(ln_g), row(ln_b))
    front_specs = [
        _full((1, D_MODEL)),
        _full((D_MODEL, IN0_WIDTH)),
        _full((1, IN0_WIDTH)),
        _full((N_KV_HEADS, rows, 1)),
        _full((CONV_KERNEL, CONV_WIDTH)),
        _full((1, CONV_WIDTH)),
        _full((1, CONV_WIDTH)),
        _full((1, CONV_WIDTH)),
    ]
    front_scratch = [
        pltpu.VMEM((tile, IN0_WIDTH), F32),
        pltpu.VMEM((N_KV_HEADS, ATTN_BLOCK + tile, LANES), BF16),
        pltpu.VMEM((N_KV_HEADS, ATTN_BLOCK + tile, LANES), BF16),
        pltpu.VMEM((CONV_HALO + tile, CONV_WIDTH), F32),
        pltpu.VMEM((tile, CONV_WIDTH), F32),
    ]
    return _layer_call(_layer0_kernel, "layer0", x, front_args, front_specs, front_scratch,
                       w_out, ffn_g, w_gate, w_up, w_down, final_g)


def _layer1_kernel(x_ref, xp_ref, g_ref, win_ref, wpool_ref, pscale_ref, lg_ref, lb_ref, ws_ref, bs_ref,
                   wout_ref, fg_ref, wg_ref, wu_ref, wd_ref, ng_ref, o_ref,
                   hn_ref, z_ref, pbuf_ref, u_ref, v_ref, wsm_ref, cat_ref, h_ref, hb_ref, act_ref,
                   *, tiles_per_seq):
    i = pl.program_id(0)
    t = i % tiles_per_seq
    slot = i % 2
    tile = x_ref.shape[0]

    @pl.when(i == 0)
    def _():
        cat_ref[...] = jnp.zeros(cat_ref.shape, BF16)

    @pl.when(t == 0)
    def _():
        pbuf_ref[0:POOL_HALO, :] = jnp.zeros((POOL_HALO, POOL_WIDTH), F32)

    @pl.when(t > 0)
    def _():
        pbuf_ref[0:POOL_HALO, :] = pbuf_ref[tile:tile + POOL_HALO, :]

    front = []

    def norm():
        hn_ref[...] = _rms(x_ref[...], g_ref[...]).astype(BF16)
        ti = lax.broadcasted_iota(jnp.int32, (SGU_CHUNK, SGU_CHUNK), 0)
        si = lax.broadcasted_iota(jnp.int32, (SGU_CHUNK, SGU_CHUNK), 1)
        for gi in range(SGU_HEADS):
            wsm_ref[gi] = jnp.where(ti >= si, ws_ref[gi], 0.0).astype(BF16)

    front.append((550, norm))

    def in_proj(c):
        cs = slice(c * MXU_COLS, (c + 1) * MXU_COLS)
        z_ref[:, cs] = jnp.dot(hn_ref[...], win_ref[:, cs], preferred_element_type=F32)

    for c in range(IN1_WIDTH // MXU_COLS):
        front.append((512, functools.partial(in_proj, c)))

    def pool_fill():
        pbuf_ref[POOL_HALO:POOL_HALO + tile, :] = z_ref[:, 0:POOL_WIDTH]

    front.append((70, pool_fill))

    def pool(gi, w):
        cs = slice(gi * POOL_GROUP, (gi + 1) * POOL_GROUP)
        pos = t * tile + lax.broadcasted_iota(jnp.int32, (tile, POOL_GROUP), 0)
        cur = pbuf_ref[POOL_HALO:POOL_HALO + tile, cs]
        s = pbuf_ref[POOL_HALO - (w - 1):POOL_HALO + tile, cs]
        span = 1
        while span < w:
            n = s.shape[0] - span
            s = s[span:span + n] + s[0:n]
            span *= 2
        cnt = jnp.minimum(pos + 1, w).astype(F32)
        pooled = s / cnt - cur
        y = jnp.dot(pooled.astype(BF16), wpool_ref[gi], preferred_element_type=F32) * pscale_ref[:, cs]
        cat_ref[slot, :, cs] = y.astype(BF16)

    for gi, w in enumerate(POOL_WINDOWS):
        front.append((250, functools.partial(pool, gi, w)))

    def gate_prep(c):
        rs = slice(c * SGU_CHUNK, (c + 1) * SGU_CHUNK)
        zz = _gelu_tanh(z_ref[rs, POOL_WIDTH:POOL_WIDTH + 2 * SGU_WIDTH])
        u_ref[rs, :] = zz[:, 0:SGU_WIDTH]
        v_ref[rs, :] = _layer_norm(zz[:, SGU_WIDTH:2 * SGU_WIDTH], lg_ref[...], lb_ref[...]).astype(BF16)

    def gate_mix(c, gi):
        rs = slice(c * SGU_CHUNK, (c + 1) * SGU_CHUNK)
        cs = slice(gi * SGU_HEAD_DIM, (gi + 1) * SGU_HEAD_DIM)
        mixed = jnp.dot(wsm_ref[gi], v_ref[rs, cs], preferred_element_type=F32) + bs_ref[:, cs]
        cat_ref[slot, rs, POOL_WIDTH + gi * SGU_HEAD_DIM:POOL_WIDTH + (gi + 1) * SGU_HEAD_DIM] = (
            u_ref[rs, cs] * mixed).astype(BF16)

    for c in range(tile // SGU_CHUNK):
        front.append((450, functools.partial(gate_prep, c)))
        for gi in range(SGU_HEADS):
            front.append((130, functools.partial(gate_mix, c, gi)))

    back = _back_stream(_Ties(), xp_ref, cat_ref.at[1 - slot], wout_ref, fg_ref, wg_ref, wu_ref, wd_ref, ng_ref,
                        o_ref, h_ref, hb_ref, act_ref, final_norm=True)
    _emit_interleaved(front, back, front_end=0.5)


def _layer1(x, g, w_in, w_pool, pool_scale, ln_g, ln_b, w_s, b_s, w_out, ffn_g, w_gate, w_up, w_down, final_g):
    tile = SEQ_TILE
    row = lambda v: v.reshape(1, -1)
    bias = jnp.repeat(b_s.T, SGU_HEAD_DIM, axis=1)
    front_args = (row(g), w_in.astype(BF16), w_pool.astype(BF16), row(pool_scale), row(ln_g), row(ln_b), w_s, bias)
    front_specs = [
        _full((1, D_MODEL)),
        _full((D_MODEL, IN1_WIDTH)),
        _full((len(POOL_WINDOWS), POOL_GROUP, POOL_GROUP)),
        _full((1, POOL_WIDTH)),
        _full((1, SGU_WIDTH)),
        _full((1, SGU_WIDTH)),
        _full((SGU_HEADS, SGU_CHUNK, SGU_CHUNK)),
        _full((SGU_CHUNK, SGU_WIDTH)),
    ]
    front_scratch = [
        pltpu.VMEM((tile, IN1_WIDTH), F32),
        pltpu.VMEM((POOL_HALO + tile, POOL_WIDTH), F32),
        pltpu.VMEM((tile, SGU_WIDTH), F32),
        pltpu.VMEM((tile, SGU_WIDTH), BF16),
        pltpu.VMEM((SGU_HEADS, SGU_CHUNK, SGU_CHUNK), BF16),
    ]
    return _layer_call(_layer1_kernel, "layer1", x, front_args, front_specs, front_scratch,
                       w_out, ffn_g, w_gate, w_up, w_down, final_g)


def kernel(x, mix_norm, a_w_in, a_b_in, a_sinks, a_conv_w, a_conv_b, a_cln_g, a_cln_b, a_w_out, c_w_in, c_w_pool, c_pool_scale, c_sln_g, c_sln_b, c_w_s, c_b_s, c_w_out, ffn_norm, ffn_w_gate, ffn_w_up, ffn_w_down, final_norm):
    h = _layer0(x, mix_norm[0], a_w_in[0], a_b_in[0], a_sinks[0], a_conv_w[0], a_conv_b[0], a_cln_g[0],
                a_cln_b[0], a_w_out[0], ffn_norm[0], ffn_w_gate[0], ffn_w_up[0], ffn_w_down[0], final_norm)
    h = _layer1(h, mix_norm[1], c_w_in[0], c_w_pool[0], c_pool_scale[0], c_sln_g[0], c_sln_b[0], c_w_s[0],
                c_b_s[0], c_w_out[0], ffn_norm[1], ffn_w_gate[1], ffn_w_up[1], ffn_w_down[1], final_norm)
    return h
```

```python
import functools

import jax
import jax.numpy as jnp
from jax import lax
from jax.experimental import pallas as pl
from jax.experimental.pallas import tpu as pltpu

F32 = jnp.float32
BF16 = jnp.bfloat16

D_MODEL = 1024
N_Q_HEADS = 8
N_KV_HEADS = 2
HEAD_DIM = 64
Q_GROUP = N_Q_HEADS // N_KV_HEADS
ATTN_BLOCK = 128
ATTN_WIDTH = N_Q_HEADS * HEAD_DIM
KV_WIDTH = N_KV_HEADS * HEAD_DIM
CONV_WIDTH = D_MODEL - ATTN_WIDTH
CONV_KERNEL = 31
IN0_WIDTH = ATTN_WIDTH + 2 * KV_WIDTH + 2 * CONV_WIDTH
POOL_WINDOWS = (2, 4, 8, 16)
POOL_WIDTH = 512
POOL_GROUP = POOL_WIDTH // len(POOL_WINDOWS)
SGU_WIDTH = 512
SGU_HEADS = 4
SGU_HEAD_DIM = SGU_WIDTH // SGU_HEADS
SGU_CHUNK = 128
IN1_WIDTH = POOL_WIDTH + 2 * SGU_WIDTH
EPS = 1e-5

LANES = 128
SUBLANES = 8
MXU_COLS = 256
CONV_HALO = 32
CONV_ROWS = 64
POOL_HALO = 16
SEQ_TILE = 512
VMEM_LIMIT = 60 * 1024 * 1024


def _rms(x, g):
    return x * lax.rsqrt(jnp.mean(x * x, axis=-1, keepdims=True) + EPS) * g


def _layer_norm(x, g, b):
    mu = jnp.mean(x, axis=-1, keepdims=True)
    xc = x - mu
    var = jnp.mean(xc * xc, axis=-1, keepdims=True)
    return xc * lax.rsqrt(var + EPS) * g + b


def _sigmoid(x):
    return 1.0 / (1.0 + jnp.exp(-x))


def _n_chunks(width):
    return -(-width // MXU_COLS)


def _cols(c, width):
    return slice(c * MXU_COLS, min((c + 1) * MXU_COLS, width))


def _gelu_tanh(x):
    c = 0.7978845608028654
    return 0.5 * x * (1.0 + jnp.tanh(c * (x + 0.044715 * (x * x * x))))


class _Ties:
    def __init__(self):
        self.pieces = []

    def add(self, value):
        for j in range(value.shape[0] // SUBLANES):
            self.pieces.append(pltpu.bitcast(value[j * SUBLANES:(j + 1) * SUBLANES], jnp.uint32))

    def bind(self, result):
        if not self.pieces:
            return result
        tile_rows = 2 * SUBLANES if result.dtype == BF16 else SUBLANES
        n_rt, n_lt = result.shape[0] // tile_rows, result.shape[1] // LANES
        zeros = []
        for piece in self.pieces:
            zero = lax.shift_right_logical(lax.shift_right_logical(piece, jnp.uint32(16)), jnp.uint32(16))
            zeros.append(pltpu.bitcast(zero, result.dtype))
        grid = []
        for rt in range(n_rt):
            row = [zeros[(rt * n_lt + lt) * len(zeros) // (n_rt * n_lt)] for lt in range(n_lt)]
            grid.append(jnp.concatenate(row, axis=1))
        self.pieces = []
        return result + jnp.concatenate(grid, axis=0)


def _emit_interleaved(front, back, front_end):
    streams = [front, back]
    pace = [front_end, 1.0]
    totals = [sum(c for c, _ in s) for s in streams]
    pos = [0] * len(streams)
    done = [0.0] * len(streams)
    while True:
        live = [k for k in range(len(streams)) if pos[k] < len(streams[k])]
        if not live:
            return
        k = min(live, key=lambda j: done[j] / totals[j] * pace[j])
        cost, thunk = streams[k][pos[k]]
        thunk()
        pos[k] += 1
        done[k] += cost


def _back_stream(ties, xp_ref, cat_view, wout_ref, fg_ref, wg_ref, wu_ref, wd_ref, ng_ref, o_ref,
                 h_ref, hb_ref, act_ref, final_norm):
    units = []
    d = D_MODEL
    d_ff = wg_ref.shape[1]

    def out_proj(c):
        cs = _cols(c, d)
        h_ref[:, cs] = xp_ref[:, cs] + jnp.dot(cat_view[...], wout_ref[:, cs], preferred_element_type=F32)

    for c in range(_n_chunks(d)):
        units.append((512, functools.partial(out_proj, c)))

    def norm():
        hb_ref[...] = _rms(h_ref[...], fg_ref[...]).astype(BF16)

    units.append((500, norm))

    def gate_up(c):
        cs = _cols(c, d_ff)
        hb = hb_ref[...]
        gate = jnp.dot(hb, wg_ref[:, cs], preferred_element_type=F32)
        up = jnp.dot(hb, wu_ref[:, cs], preferred_element_type=F32)
        act_ref[:, cs] = ties.bind((gate * _sigmoid(gate) * up).astype(BF16))

    for c in range(_n_chunks(d_ff)):
        units.append((1024, functools.partial(gate_up, c)))

    def down(c):
        cs = _cols(c, d)
        y = ties.bind(h_ref[:, cs] + jnp.dot(act_ref[...], wd_ref[:, cs], preferred_element_type=F32))
        if final_norm:
            h_ref[:, cs] = y
        else:
            o_ref[:, cs] = y

    for c in range(_n_chunks(d)):
        units.append((1408, functools.partial(down, c)))

    if final_norm:
        def last():
            o_ref[...] = _rms(h_ref[...], ng_ref[...])

        units.append((500, last))
    return units


def _layer0_kernel(x_ref, xp_ref, g_ref, win_ref, bin_ref, sink_ref, cw_ref, cb_ref, lg_ref, lb_ref,
                   wout_ref, fg_ref, wg_ref, wu_ref, wd_ref, ng_ref, o_ref,
                   hn_ref, z_ref, kd_ref, vd_ref, hbuf_ref, conv_ref, cat_ref, h_ref, hb_ref, act_ref,
                   *, tiles_per_seq):
    i = pl.program_id(0)
    new = i % 2
    old = 1 - new
    t_new = i % tiles_per_seq
    tile = x_ref.shape[0]
    blk = ATTN_BLOCK
    ties = _Ties()

    @pl.when(i == 0)
    def _():
        cat_ref[...] = jnp.zeros(cat_ref.shape, BF16)

    @pl.when(t_new == 0)
    def _():
        kd_ref[:, 0:blk, :] = jnp.zeros((N_KV_HEADS, blk, LANES), BF16)
        vd_ref[:, 0:blk, :] = jnp.zeros((N_KV_HEADS, blk, LANES), BF16)

    @pl.when(t_new > 0)
    def _():
        kd_ref[:, 0:blk, :] = kd_ref[:, tile:tile + blk, :]
        vd_ref[:, 0:blk, :] = vd_ref[:, tile:tile + blk, :]

    @pl.when(t_new == 0)
    def _():
        hbuf_ref[0:CONV_HALO, :] = jnp.zeros((CONV_HALO, CONV_WIDTH), F32)

    @pl.when(t_new > 0)
    def _():
        hbuf_ref[0:CONV_HALO, :] = hbuf_ref[tile:tile + CONV_HALO, :]

    front = []

    def norm():
        hn_ref[...] = _rms(x_ref[...], g_ref[...]).astype(BF16)

    def in_proj(c):
        cs = _cols(c, IN0_WIDTH)
        z_ref[:, cs] = jnp.dot(hn_ref[...], win_ref[:, cs], preferred_element_type=F32) + bin_ref[:, cs]

    def kv_dup():
        lo_t = lax.broadcasted_iota(jnp.int32, (tile, LANES), 1) < HEAD_DIM
        zk = z_ref[:, ATTN_WIDTH:ATTN_WIDTH + KV_WIDTH]
        zv = z_ref[:, ATTN_WIDTH + KV_WIDTH:ATTN_WIDTH + 2 * KV_WIDTH]
        zk_sw = pltpu.roll(zk, HEAD_DIM, 1)
        zv_sw = pltpu.roll(zv, HEAD_DIM, 1)
        kd_ref[0, blk:blk + tile, :] = jnp.where(lo_t, zk, zk_sw).astype(BF16)
        kd_ref[1, blk:blk + tile, :] = jnp.where(lo_t, zk_sw, zk).astype(BF16)
        vd_ref[0, blk:blk + tile, :] = jnp.where(lo_t, zv, zv_sw).astype(BF16)
        vd_ref[1, blk:blk + tile, :] = jnp.where(lo_t, zv_sw, zv).astype(BF16)

    def glu(c):
        rs = slice(c * CONV_ROWS, (c + 1) * CONV_ROWS)
        c0 = ATTN_WIDTH + 2 * KV_WIDTH
        a = z_ref[rs, c0:c0 + CONV_WIDTH]
        gate = z_ref[rs, c0 + CONV_WIDTH:c0 + 2 * CONV_WIDTH]
        hbuf_ref[CONV_HALO + c * CONV_ROWS:CONV_HALO + (c + 1) * CONV_ROWS, :] = a * _sigmoid(gate)

    rows = Q_GROUP * blk
    scale = HEAD_DIM ** -0.5

    def attn_block(b, kh):
        qi = lax.broadcasted_iota(jnp.int32, (rows, blk), 0) % blk
        col = lax.broadcasted_iota(jnp.int32, (rows, blk), 1)
        from_prev = col > qi
        lo = lax.broadcasted_iota(jnp.int32, (blk, LANES), 1) < HEAD_DIM
        rs = slice(b * blk, (b + 1) * blk)
        parts = []
        for j in range(Q_GROUP // 2):
            c0 = kh * Q_GROUP * HEAD_DIM + j * LANES
            qp = z_ref[rs, c0:c0 + LANES] * scale
            parts.append(jnp.where(lo, qp, 0.0))
            parts.append(jnp.where(lo, 0.0, qp))
        qs = jnp.concatenate(parts, axis=0).astype(BF16)
        kw = kd_ref[kh, b * blk:(b + 2) * blk, :]
        both = lax.dot_general(qs, kw, (((1,), (1,)), ((), ())), preferred_element_type=F32)
        logits = jnp.where(from_prev, both[:, 0:blk], both[:, blk:2 * blk])
        if b == 0:
            logits = jnp.where(from_prev & (t_new == 0), -jnp.inf, logits)
        sink = sink_ref[kh]
        m = jnp.maximum(jnp.max(logits, axis=-1, keepdims=True), sink)
        p = jnp.exp(logits - m)
        den = jnp.sum(p, axis=-1, keepdims=True) + jnp.exp(sink - m)
        pw = jnp.concatenate([jnp.where(from_prev, p, 0.0), jnp.where(from_prev, 0.0, p)], axis=1)
        vw = vd_ref[kh, b * blk:(b + 2) * blk, :]
        o = jnp.dot(pw.astype(BF16), vw, preferred_element_type=F32) / den
        for j in range(Q_GROUP // 2):
            pair = jnp.where(lo, o[2 * j * blk:(2 * j + 1) * blk], o[(2 * j + 1) * blk:(2 * j + 2) * blk])
            c0 = kh * Q_GROUP * HEAD_DIM + j * LANES
            cat_ref[new, rs, c0:c0 + LANES] = pair.astype(BF16)

    def conv_slab(c, cg):
        rs = slice(c * CONV_ROWS, (c + 1) * CONV_ROWS)
        ls = slice(cg * LANES, (cg + 1) * LANES)
        depth = (CONV_KERNEL - 1) // SUBLANES
        top = CONV_HALO + c * CONV_ROWS - (depth + 1) * SUBLANES
        win = hbuf_ref[top:top + (depth + 1) * SUBLANES + CONV_ROWS, ls]
        acc = None
        for r in range(SUBLANES):
            pad = SUBLANES if r else 0
            n = CONV_ROWS + pad
            part = None
            for a in range((CONV_KERNEL - 1 - r) // SUBLANES + 1):
                k = CONV_KERNEL - 1 - (SUBLANES * a + r)
                src = (depth + 1 - a) * SUBLANES - pad
                term = cw_ref[k:k + 1, ls] * win[src:src + n]
                part = term if part is None else part + term
            if r:
                part = part[SUBLANES - r:SUBLANES - r + CONV_ROWS]
            acc = part if acc is None else acc + part
        conv_ref[rs, ls] = acc + cb_ref[:, ls]
        ties.add(acc)

    def conv_norm(c):
        rs = slice(c * CONV_ROWS, (c + 1) * CONV_ROWS)
        y = _layer_norm(conv_ref[rs, :], lg_ref[...], lb_ref[...])
        y = y * _sigmoid(y)
        cat_ref[new, rs, ATTN_WIDTH:ATTN_WIDTH + CONV_WIDTH] = y.astype(BF16)
        for cg in range(CONV_WIDTH // LANES):
            ties.add(y[:, cg * LANES:(cg + 1) * LANES])

    front.append((500, norm))
    for c in range(_n_chunks(IN0_WIDTH)):
        front.append((512, functools.partial(in_proj, c)))
    front.append((150, kv_dup))
    for c in range(tile // CONV_ROWS):
        front.append((60, functools.partial(glu, c)))
    steps = [(b, kh) for b in range(tile // blk) for kh in range(N_KV_HEADS)]
    slabs = [(c, cg) for c in range(tile // CONV_ROWS) for cg in range(CONV_WIDTH // LANES)]
    slabs_per_step = len(slabs) // len(steps)
    for n, (b, kh) in enumerate(steps):
        front.append((700, functools.partial(attn_block, b, kh)))
        for c, cg in slabs[n * slabs_per_step:(n + 1) * slabs_per_step]:
            front.append((170, functools.partial(conv_slab, c, cg)))
            if cg == CONV_WIDTH // LANES - 1:
                front.append((150, functools.partial(conv_norm, c)))

    back = _back_stream(ties, xp_ref, cat_ref.at[old], wout_ref, fg_ref, wg_ref, wu_ref, wd_ref, ng_ref, o_ref,
                        h_ref, hb_ref, act_ref, final_norm=False)
    _emit_interleaved(front, back, front_end=0.75)


def _full(shape):
    return pl.BlockSpec(shape, lambda *_: (0,) * len(shape))


def _layer_call(body, name, x, front_args, front_specs, front_scratch, w_out, ffn_g, w_gate, w_up, w_down,
                final_g):
    B, S, D = x.shape
    M = B * S
    tile = SEQ_TILE
    n_tiles = M // tile
    d_ff = w_gate.shape[1]
    row = lambda v: v.reshape(1, -1)
    out = pl.pallas_call(
        functools.partial(body, tiles_per_seq=S // tile),
        name=name,
        grid=(n_tiles + 1,),
        in_specs=[
            pl.BlockSpec((tile, D), lambda i: (jnp.minimum(i, n_tiles - 1), 0)),
            pl.BlockSpec((tile, D), lambda i: (jnp.maximum(i - 1, 0), 0)),
            *front_specs,
            _full((D, D)),
            _full((1, D)),
            _full((D, d_ff)),
            _full((D, d_ff)),
            _full((d_ff, D)),
            _full((1, D)),
        ],
        out_specs=pl.BlockSpec((tile, D), lambda i: (jnp.maximum(i - 1, 0), 0)),
        out_shape=jax.ShapeDtypeStruct((M, D), F32),
        scratch_shapes=[
            pltpu.VMEM((tile, D), BF16),
            *front_scratch,
            pltpu.VMEM((2, tile, D), BF16),
            pltpu.VMEM((tile, D), F32),
            pltpu.VMEM((tile, D), BF16),
            pltpu.VMEM((tile, d_ff), BF16),
        ],
        compiler_params=pltpu.CompilerParams(
            dimension_semantics=("arbitrary",), vmem_limit_bytes=VMEM_LIMIT),
    )(x.reshape(M, D), x.reshape(M, D), *front_args, w_out.astype(BF16), row(ffn_g), w_gate.astype(BF16),
      w_up.astype(BF16), w_down.astype(BF16), row(final_g))
    return out.reshape(B, S, D)


def _layer0(x, g, w_in, b_in, sinks, conv_w, conv_b, ln_g, ln_b, w_out, ffn_g, w_gate, w_up, w_down, final_g):
    tile = SEQ_TILE
    rows = Q_GROUP * ATTN_BLOCK
    row = lambda v: v.reshape(1, -1)
    sink_cols = jnp.repeat(sinks.reshape(N_KV_HEADS, Q_GROUP), ATTN_BLOCK, axis=1).reshape(N_KV_HEADS, rows, 1)
    front_args = (row(g), w_in.astype(BF16), row(b_in), sink_cols, conv_w, row(conv_b), row(ln_g), row(ln_b))
    front_specs = [
        _full((1, D_MODEL)),
        _full((D_MODEL, IN0_WIDTH)),
        _full((1, IN0_WIDTH)),
        _full((N_KV_HEADS, rows, 1)),
        _full((CONV_KERNEL, CONV_WIDTH)),
        _full((1, CONV_WIDTH)),
        _full((1, CONV_WIDTH)),
        _full((1, CONV_WIDTH)),
    ]
    front_scratch = [
        pltpu.VMEM((tile, IN0_WIDTH), F32),
        pltpu.VMEM((N_KV_HEADS, ATTN_BLOCK + tile, LANES), BF16),
        pltpu.VMEM((N_KV_HEADS, ATTN_BLOCK + tile, LANES), BF16),
        pltpu.VMEM((CONV_HALO + tile, CONV_WIDTH), F32),
        pltpu.VMEM((tile, CONV_WIDTH), F32),
    ]
    return _layer_call(_layer0_kernel, "layer0", x, front_args, front_specs, front_scratch,
                       w_out, ffn_g, w_gate, w_up, w_down, final_g)


def _layer1_kernel(x_ref, xp_ref, g_ref, win_ref, wpool_ref, pscale_ref, lg_ref, lb_ref, ws_ref, bs_ref,
                   wout_ref, fg_ref, wg_ref, wu_ref, wd_ref, ng_ref, o_ref,
                   hn_ref, z_ref, pbuf_ref, u_ref, v_ref, wsm_ref, cat_ref, h_ref, hb_ref, act_ref,
                   *, tiles_per_seq):
    i = pl.program_id(0)
    t = i % tiles_per_seq
    slot = i % 2
    tile = x_ref.shape[0]

    @pl.when(i == 0)
    def _():
        cat_ref[...] = jnp.zeros(cat_ref.shape, BF16)

    @pl.when(t == 0)
    def _():
        pbuf_ref[0:POOL_HALO, :] = jnp.zeros((POOL_HALO, POOL_WIDTH), F32)

    @pl.when(t > 0)
    def _():
        pbuf_ref[0:POOL_HALO, :] = pbuf_ref[tile:tile + POOL_HALO, :]

    front = []

    def norm():
        hn_ref[...] = _rms(x_ref[...], g_ref[...]).astype(BF16)
        ti = lax.broadcasted_iota(jnp.int32, (SGU_CHUNK, SGU_CHUNK), 0)
        si = lax.broadcasted_iota(jnp.int32, (SGU_CHUNK, SGU_CHUNK), 1)
        for gi in range(SGU_HEADS):
            wsm_ref[gi] = jnp.where(ti >= si, ws_ref[gi], 0.0).astype(BF16)

    front.append((550, norm))

    def in_proj(c):
        cs = _cols(c, IN1_WIDTH)
        z_ref[:, cs] = jnp.dot(hn_ref[...], win_ref[:, cs], preferred_element_type=F32)

    for c in range(_n_chunks(IN1_WIDTH)):
        front.append((512, functools.partial(in_proj, c)))

    def pool_fill():
        pbuf_ref[POOL_HALO:POOL_HALO + tile, :] = z_ref[:, 0:POOL_WIDTH]

    front.append((70, pool_fill))

    def pool(gi, w):
        cs = slice(gi * POOL_GROUP, (gi + 1) * POOL_GROUP)
        pos = t * tile + lax.broadcasted_iota(jnp.int32, (tile, POOL_GROUP), 0)
        cur = pbuf_ref[POOL_HALO:POOL_HALO + tile, cs]
        s = pbuf_ref[POOL_HALO - (w - 1):POOL_HALO + tile, cs]
        span = 1
        while span < w:
            n = s.shape[0] - span
            s = s[span:span + n] + s[0:n]
            span *= 2
        cnt = jnp.minimum(pos + 1, w).astype(F32)
        pooled = s / cnt - cur
        y = jnp.dot(pooled.astype(BF16), wpool_ref[gi], preferred_element_type=F32) * pscale_ref[:, cs]
        cat_ref[slot, :, cs] = y.astype(BF16)

    for gi, w in enumerate(POOL_WINDOWS):
        front.append((250, functools.partial(pool, gi, w)))

    def gate_prep(c):
        rs = slice(c * SGU_CHUNK, (c + 1) * SGU_CHUNK)
        zz = _gelu_tanh(z_ref[rs, POOL_WIDTH:POOL_WIDTH + 2 * SGU_WIDTH])
        u_ref[rs, :] = zz[:, 0:SGU_WIDTH]
        v_ref[rs, :] = _layer_norm(zz[:, SGU_WIDTH:2 * SGU_WIDTH], lg_ref[...], lb_ref[...]).astype(BF16)

    def gate_mix(gi):
        cs = slice(gi * SGU_HEAD_DIM, (gi + 1) * SGU_HEAD_DIM)
        n_chunks = tile // SGU_CHUNK
        v_all = jnp.concatenate([v_ref[c * SGU_CHUNK:(c + 1) * SGU_CHUNK, cs] for c in range(n_chunks)], axis=1)
        mixed_all = jnp.dot(wsm_ref[gi], v_all, preferred_element_type=F32)
        for c in range(n_chunks):
            rs = slice(c * SGU_CHUNK, (c + 1) * SGU_CHUNK)
            mixed = mixed_all[:, c * SGU_HEAD_DIM:(c + 1) * SGU_HEAD_DIM] + bs_ref[:, cs]
            cat_ref[slot, rs, POOL_WIDTH + gi * SGU_HEAD_DIM:POOL_WIDTH + (gi + 1) * SGU_HEAD_DIM] = (
                u_ref[rs, cs] * mixed).astype(BF16)

    for c in range(tile // SGU_CHUNK):
        front.append((450, functools.partial(gate_prep, c)))
    for gi in range(SGU_HEADS):
        front.append((520, functools.partial(gate_mix, gi)))

    back = _back_stream(_Ties(), xp_ref, cat_ref.at[1 - slot], wout_ref, fg_ref, wg_ref, wu_ref, wd_ref, ng_ref,
                        o_ref, h_ref, hb_ref, act_ref, final_norm=True)
    _emit_interleaved(front, back, front_end=0.5)


def _layer1(x, g, w_in, w_pool, pool_scale, ln_g, ln_b, w_s, b_s, w_out, ffn_g, w_gate, w_up, w_down, final_g):
    tile = SEQ_TILE
    row = lambda v: v.reshape(1, -1)
    bias = jnp.repeat(b_s.T, SGU_HEAD_DIM, axis=1)
    front_args = (row(g), w_in.astype(BF16), w_pool.astype(BF16), row(pool_scale), row(ln_g), row(ln_b), w_s, bias)
    front_specs = [
        _full((1, D_MODEL)),
        _full((D_MODEL, IN1_WIDTH)),
        _full((len(POOL_WINDOWS), POOL_GROUP, POOL_GROUP)),
        _full((1, POOL_WIDTH)),
        _full((1, SGU_WIDTH)),
        _full((1, SGU_WIDTH)),
        _full((SGU_HEADS, SGU_CHUNK, SGU_CHUNK)),
        _full((SGU_CHUNK, SGU_WIDTH)),
    ]
    front_scratch = [
        pltpu.VMEM((tile, IN1_WIDTH), F32),
        pltpu.VMEM((POOL_HALO + tile, POOL_WIDTH), F32),
        pltpu.VMEM((tile, SGU_WIDTH), F32),
        pltpu.VMEM((tile, SGU_WIDTH), BF16),
        pltpu.VMEM((SGU_HEADS, SGU_CHUNK, SGU_CHUNK), BF16),
    ]
    return _layer_call(_layer1_kernel, "layer1", x, front_args, front_specs, front_scratch,
                       w_out, ffn_g, w_gate, w_up, w_down, final_g)


def kernel(x, mix_norm, a_w_in, a_b_in, a_sinks, a_conv_w, a_conv_b, a_cln_g, a_cln_b, a_w_out, c_w_in, c_w_pool, c_pool_scale, c_sln_g, c_sln_b, c_w_s, c_b_s, c_w_out, ffn_norm, ffn_w_gate, ffn_w_up, ffn_w_down, final_norm):
    h = _layer0(x, mix_norm[0], a_w_in[0], a_b_in[0], a_sinks[0], a_conv_w[0], a_conv_b[0], a_cln_g[0],
                a_cln_b[0], a_w_out[0], ffn_norm[0], ffn_w_gate[0], ffn_w_up[0], ffn_w_down[0], final_norm)
    h = _layer1(h, mix_norm[1], c_w_in[0], c_w_pool[0], c_pool_scale[0], c_sln_g[0], c_sln_b[0], c_w_s[0],
                c_b_s[0], c_w_out[0], ffn_norm[1], ffn_w_gate[1], ffn_w_up[1], ffn_w_down[1], final_norm)
    return h
```

```python
import functools

import jax
import jax.numpy as jnp
from jax import lax
from jax.experimental import pallas as pl
from jax.experimental.pallas import tpu as pltpu

F32 = jnp.float32
BF16 = jnp.bfloat16

D_MODEL = 1024
N_Q_HEADS = 8
N_KV_HEADS = 2
HEAD_DIM = 64
Q_GROUP = N_Q_HEADS // N_KV_HEADS
ATTN_BLOCK = 128
ATTN_WIDTH = N_Q_HEADS * HEAD_DIM
KV_WIDTH = N_KV_HEADS * HEAD_DIM
CONV_WIDTH = D_MODEL - ATTN_WIDTH
CONV_KERNEL = 31
IN0_WIDTH = ATTN_WIDTH + 2 * KV_WIDTH + 2 * CONV_WIDTH
POOL_WINDOWS = (2, 4, 8, 16)
POOL_WIDTH = 512
POOL_GROUP = POOL_WIDTH // len(POOL_WINDOWS)
SGU_WIDTH = 512
SGU_HEADS = 4
SGU_HEAD_DIM = SGU_WIDTH // SGU_HEADS
SGU_CHUNK = 128
IN1_WIDTH = POOL_WIDTH + 2 * SGU_WIDTH
EPS = 1e-5

LANES = 128
SUBLANES = 8
MXU_COLS = 256
CONV_HALO = 32
CONV_ROWS = 64
POOL_HALO = 16
SEQ_TILE = 512
VMEM_LIMIT = 60 * 1024 * 1024


def _rms(x, g):
    return x * lax.rsqrt(jnp.mean(x * x, axis=-1, keepdims=True) + EPS) * g


def _layer_norm(x, g, b):
    mu = jnp.mean(x, axis=-1, keepdims=True)
    xc = x - mu
    var = jnp.mean(xc * xc, axis=-1, keepdims=True)
    return xc * lax.rsqrt(var + EPS) * g + b


def _sigmoid(x):
    return 1.0 / (1.0 + jnp.exp(-x))


def _n_chunks(width):
    return -(-width // MXU_COLS)


def _cols(c, width):
    return slice(c * MXU_COLS, min((c + 1) * MXU_COLS, width))


def _gelu_tanh(x):
    c = 0.7978845608028654
    return 0.5 * x * (1.0 + jnp.tanh(c * (x + 0.044715 * (x * x * x))))


class _Ties:
    def __init__(self):
        self.pieces = []

    def add(self, value, n_pieces):
        n_tiles = value.shape[0] // SUBLANES
        per = n_tiles // n_pieces
        for p in range(n_pieces):
            word = None
            for j in range(p * per, (p + 1) * per):
                tile_bits = pltpu.bitcast(value[j * SUBLANES:(j + 1) * SUBLANES], jnp.uint32)
                word = tile_bits if word is None else word | tile_bits
            self.pieces.append(word)

    def bind(self, result):
        if not self.pieces:
            return result
        tile_rows = 2 * SUBLANES if result.dtype == BF16 else SUBLANES
        n_rt, n_lt = result.shape[0] // tile_rows, result.shape[1] // LANES
        zeros = []
        for piece in self.pieces:
            zero = lax.shift_right_logical(lax.shift_right_logical(piece, jnp.uint32(16)), jnp.uint32(16))
            zeros.append(pltpu.bitcast(zero, result.dtype))
        grid = []
        for rt in range(n_rt):
            row = [zeros[(rt * n_lt + lt) * len(zeros) // (n_rt * n_lt)] for lt in range(n_lt)]
            grid.append(jnp.concatenate(row, axis=1))
        self.pieces = []
        return result + jnp.concatenate(grid, axis=0)


def _emit_interleaved(front, back, front_end):
    streams = [front, back]
    pace = [front_end, 1.0]
    totals = [sum(c for c, _ in s) for s in streams]
    pos = [0] * len(streams)
    done = [0.0] * len(streams)
    while True:
        live = [k for k in range(len(streams)) if pos[k] < len(streams[k])]
        if not live:
            return
        k = min(live, key=lambda j: done[j] / totals[j] * pace[j])
        cost, thunk = streams[k][pos[k]]
        thunk()
        pos[k] += 1
        done[k] += cost


def _back_stream(ties, xp_ref, cat_view, wout_ref, fg_ref, wg_ref, wu_ref, wd_ref, ng_ref, o_ref,
                 h_ref, hb_ref, act_ref, final_norm):
    units = []
    d = D_MODEL
    d_ff = wg_ref.shape[1]

    def out_proj(c):
        cs = _cols(c, d)
        h_ref[:, cs] = xp_ref[:, cs] + jnp.dot(cat_view[...], wout_ref[:, cs], preferred_element_type=F32)

    for c in range(_n_chunks(d)):
        units.append((512, functools.partial(out_proj, c)))

    def norm():
        hb_ref[...] = _rms(h_ref[...], fg_ref[...]).astype(BF16)

    units.append((500, norm))

    def gate_up(c):
        cs = _cols(c, d_ff)
        hb = hb_ref[...]
        gate = jnp.dot(hb, wg_ref[:, cs], preferred_element_type=F32)
        up = jnp.dot(hb, wu_ref[:, cs], preferred_element_type=F32)
        act_ref[:, cs] = ties.bind((gate * _sigmoid(gate) * up).astype(BF16))

    for c in range(_n_chunks(d_ff)):
        units.append((1024, functools.partial(gate_up, c)))

    def down(c):
        cs = _cols(c, d)
        y = ties.bind(h_ref[:, cs] + jnp.dot(act_ref[...], wd_ref[:, cs], preferred_element_type=F32))
        if final_norm:
            h_ref[:, cs] = y
        else:
            o_ref[:, cs] = y

    for c in range(_n_chunks(d)):
        units.append((1408, functools.partial(down, c)))

    if final_norm:
        def last():
            o_ref[...] = _rms(h_ref[...], ng_ref[...])

        units.append((500, last))
    return units


def _layer0_kernel(x_ref, xp_ref, g_ref, win_ref, bin_ref, sink_ref, cw_ref, cb_ref, lg_ref, lb_ref,
                   wout_ref, fg_ref, wg_ref, wu_ref, wd_ref, ng_ref, o_ref,
                   hn_ref, z_ref, kd_ref, vd_ref, hbuf_ref, conv_ref, cat_ref, h_ref, hb_ref, act_ref,
                   *, tiles_per_seq):
    i = pl.program_id(0)
    new = i % 2
    old = 1 - new
    t_new = i % tiles_per_seq
    tile = x_ref.shape[0]
    blk = ATTN_BLOCK
    ties = _Ties()

    @pl.when(i == 0)
    def _():
        cat_ref[...] = jnp.zeros(cat_ref.shape, BF16)

    @pl.when(t_new == 0)
    def _():
        kd_ref[:, 0:blk, :] = jnp.zeros((N_KV_HEADS, blk, LANES), BF16)
        vd_ref[:, 0:blk, :] = jnp.zeros((N_KV_HEADS, blk, LANES), BF16)

    @pl.when(t_new > 0)
    def _():
        kd_ref[:, 0:blk, :] = kd_ref[:, tile:tile + blk, :]
        vd_ref[:, 0:blk, :] = vd_ref[:, tile:tile + blk, :]

    @pl.when(t_new == 0)
    def _():
        hbuf_ref[0:CONV_HALO, :] = jnp.zeros((CONV_HALO, CONV_WIDTH), F32)

    @pl.when(t_new > 0)
    def _():
        hbuf_ref[0:CONV_HALO, :] = hbuf_ref[tile:tile + CONV_HALO, :]

    front = []

    def norm():
        hn_ref[...] = _rms(x_ref[...], g_ref[...]).astype(BF16)

    def in_proj(c):
        cs = _cols(c, IN0_WIDTH)
        z_ref[:, cs] = jnp.dot(hn_ref[...], win_ref[:, cs], preferred_element_type=F32) + bin_ref[:, cs]

    def kv_dup():
        lo_t = lax.broadcasted_iota(jnp.int32, (tile, LANES), 1) < HEAD_DIM
        zk = z_ref[:, ATTN_WIDTH:ATTN_WIDTH + KV_WIDTH]
        zv = z_ref[:, ATTN_WIDTH + KV_WIDTH:ATTN_WIDTH + 2 * KV_WIDTH]
        zk_sw = pltpu.roll(zk, HEAD_DIM, 1)
        zv_sw = pltpu.roll(zv, HEAD_DIM, 1)
        kd_ref[0, blk:blk + tile, :] = jnp.where(lo_t, zk, zk_sw).astype(BF16)
        kd_ref[1, blk:blk + tile, :] = jnp.where(lo_t, zk_sw, zk).astype(BF16)
        vd_ref[0, blk:blk + tile, :] = jnp.where(lo_t, zv, zv_sw).astype(BF16)
        vd_ref[1, blk:blk + tile, :] = jnp.where(lo_t, zv_sw, zv).astype(BF16)

    def glu(c):
        rs = slice(c * CONV_ROWS, (c + 1) * CONV_ROWS)
        c0 = ATTN_WIDTH + 2 * KV_WIDTH
        a = z_ref[rs, c0:c0 + CONV_WIDTH]
        gate = z_ref[rs, c0 + CONV_WIDTH:c0 + 2 * CONV_WIDTH]
        hbuf_ref[CONV_HALO + c * CONV_ROWS:CONV_HALO + (c + 1) * CONV_ROWS, :] = a * _sigmoid(gate)

    rows = Q_GROUP * blk
    scale = HEAD_DIM ** -0.5

    def attn_block(b, kh):
        qi = lax.broadcasted_iota(jnp.int32, (rows, blk), 0) % blk
        col = lax.broadcasted_iota(jnp.int32, (rows, blk), 1)
        from_prev = col > qi
        lo = lax.broadcasted_iota(jnp.int32, (blk, LANES), 1) < HEAD_DIM
        rs = slice(b * blk, (b + 1) * blk)
        parts = []
        for j in range(Q_GROUP // 2):
            c0 = kh * Q_GROUP * HEAD_DIM + j * LANES
            qp = z_ref[rs, c0:c0 + LANES] * scale
            parts.append(jnp.where(lo, qp, 0.0))
            parts.append(jnp.where(lo, 0.0, qp))
        qs = jnp.concatenate(parts, axis=0).astype(BF16)
        kw = kd_ref[kh, b * blk:(b + 2) * blk, :]
        both = lax.dot_general(qs, kw, (((1,), (1,)), ((), ())), preferred_element_type=F32)
        logits = jnp.where(from_prev, both[:, 0:blk], both[:, blk:2 * blk])
        if b == 0:
            logits = jnp.where(from_prev & (t_new == 0), -jnp.inf, logits)
        sink = sink_ref[kh]
        m = jnp.maximum(jnp.max(logits, axis=-1, keepdims=True), sink)
        p = jnp.exp(logits - m)
        den = jnp.sum(p, axis=-1, keepdims=True) + jnp.exp(sink - m)
        pw = jnp.concatenate([jnp.where(from_prev, p, 0.0), jnp.where(from_prev, 0.0, p)], axis=1)
        vw = vd_ref[kh, b * blk:(b + 2) * blk, :]
        o = jnp.dot(pw.astype(BF16), vw, preferred_element_type=F32) / den
        for j in range(Q_GROUP // 2):
            pair = jnp.where(lo, o[2 * j * blk:(2 * j + 1) * blk], o[(2 * j + 1) * blk:(2 * j + 2) * blk])
            c0 = kh * Q_GROUP * HEAD_DIM + j * LANES
            cat_ref[new, rs, c0:c0 + LANES] = pair.astype(BF16)

    def conv_slab(c, cg):
        rs = slice(c * CONV_ROWS, (c + 1) * CONV_ROWS)
        ls = slice(cg * LANES, (cg + 1) * LANES)
        depth = (CONV_KERNEL - 1) // SUBLANES
        top = CONV_HALO + c * CONV_ROWS - (depth + 1) * SUBLANES
        win = hbuf_ref[top:top + (depth + 1) * SUBLANES + CONV_ROWS, ls]
        acc = None
        for r in range(SUBLANES):
            pad = SUBLANES if r else 0
            n = CONV_ROWS + pad
            part = None
            for a in range((CONV_KERNEL - 1 - r) // SUBLANES + 1):
                k = CONV_KERNEL - 1 - (SUBLANES * a + r)
                src = (depth + 1 - a) * SUBLANES - pad
                term = cw_ref[k:k + 1, ls] * win[src:src + n]
                part = term if part is None else part + term
            if r:
                part = part[SUBLANES - r:SUBLANES - r + CONV_ROWS]
            acc = part if acc is None else acc + part
        conv_ref[rs, ls] = acc + cb_ref[:, ls]
        ties.add(acc, 4)

    def conv_norm(c):
        rs = slice(c * CONV_ROWS, (c + 1) * CONV_ROWS)
        y = _layer_norm(conv_ref[rs, :], lg_ref[...], lb_ref[...])
        y = y * _sigmoid(y)
        cat_ref[new, rs, ATTN_WIDTH:ATTN_WIDTH + CONV_WIDTH] = y.astype(BF16)
        for cg in range(CONV_WIDTH // LANES):
            ties.add(y[:, cg * LANES:(cg + 1) * LANES], 2)

    front.append((500, norm))
    for c in range(_n_chunks(IN0_WIDTH)):
        front.append((512, functools.partial(in_proj, c)))
    front.append((150, kv_dup))
    for c in range(tile // CONV_ROWS):
        front.append((60, functools.partial(glu, c)))
    steps = [(b, kh) for b in range(tile // blk) for kh in range(N_KV_HEADS)]
    slabs = [(c, cg) for c in range(tile // CONV_ROWS) for cg in range(CONV_WIDTH // LANES)]
    slabs_per_step = len(slabs) // len(steps)
    for n, (b, kh) in enumerate(steps):
        front.append((700, functools.partial(attn_block, b, kh)))
        for c, cg in slabs[n * slabs_per_step:(n + 1) * slabs_per_step]:
            front.append((170, functools.partial(conv_slab, c, cg)))
            if cg == CONV_WIDTH // LANES - 1:
                front.append((150, functools.partial(conv_norm, c)))

    back = _back_stream(ties, xp_ref, cat_ref.at[old], wout_ref, fg_ref, wg_ref, wu_ref, wd_ref, ng_ref, o_ref,
                        h_ref, hb_ref, act_ref, final_norm=False)
    _emit_interleaved(front, back, front_end=0.75)


def _full(shape):
    return pl.BlockSpec(shape, lambda *_: (0,) * len(shape))


def _layer_call(body, name, x, front_args, front_specs, front_scratch, w_out, ffn_g, w_gate, w_up, w_down,
                final_g):
    B, S, D = x.shape
    M = B * S
    tile = SEQ_TILE
    n_tiles = M // tile
    d_ff = w_gate.shape[1]
    row = lambda v: v.reshape(1, -1)
    out = pl.pallas_call(
        functools.partial(body, tiles_per_seq=S // tile),
        name=name,
        grid=(n_tiles + 1,),
        in_specs=[
            pl.BlockSpec((tile, D), lambda i: (jnp.minimum(i, n_tiles - 1), 0)),
            pl.BlockSpec((tile, D), lambda i: (jnp.maximum(i - 1, 0), 0)),
            *front_specs,
            _full((D, D)),
            _full((1, D)),
            _full((D, d_ff)),
            _full((D, d_ff)),
            _full((d_ff, D)),
            _full((1, D)),
        ],
        out_specs=pl.BlockSpec((tile, D), lambda i: (jnp.maximum(i - 1, 0), 0)),
        out_shape=jax.ShapeDtypeStruct((M, D), F32),
        scratch_shapes=[
            pltpu.VMEM((tile, D), BF16),
            *front_scratch,
            pltpu.VMEM((2, tile, D), BF16),
            pltpu.VMEM((tile, D), F32),
            pltpu.VMEM((tile, D), BF16),
            pltpu.VMEM((tile, d_ff), BF16),
        ],
        compiler_params=pltpu.CompilerParams(
            dimension_semantics=("arbitrary",), vmem_limit_bytes=VMEM_LIMIT),
    )(x.reshape(M, D), x.reshape(M, D), *front_args, w_out.astype(BF16), row(ffn_g), w_gate.astype(BF16),
      w_up.astype(BF16), w_down.astype(BF16), row(final_g))
    return out.reshape(B, S, D)


def _layer0(x, g, w_in, b_in, sinks, conv_w, conv_b, ln_g, ln_b, w_out, ffn_g, w_gate, w_up, w_down, final_g):
    tile = SEQ_TILE
    rows = Q_GROUP * ATTN_BLOCK
    row = lambda v: v.reshape(1, -1)
    sink_cols = jnp.repeat(sinks.reshape(N_KV_HEADS, Q_GROUP), ATTN_BLOCK, axis=1).reshape(N_KV_HEADS, rows, 1)
    front_args = (row(g), w_in.astype(BF16), row(b_in), sink_cols, conv_w, row(conv_b), row(ln_g), row(ln_b))
    front_specs = [
        _full((1, D_MODEL)),
        _full((D_MODEL, IN0_WIDTH)),
        _full((1, IN0_WIDTH)),
        _full((N_KV_HEADS, rows, 1)),
        _full((CONV_KERNEL, CONV_WIDTH)),
        _full((1, CONV_WIDTH)),
        _full((1, CONV_WIDTH)),
        _full((1, CONV_WIDTH)),
    ]
    front_scratch = [
        pltpu.VMEM((tile, IN0_WIDTH), F32),
        pltpu.VMEM((N_KV_HEADS, ATTN_BLOCK + tile, LANES), BF16),
        pltpu.VMEM((N_KV_HEADS, ATTN_BLOCK + tile, LANES), BF16),
        pltpu.VMEM((CONV_HALO + tile, CONV_WIDTH), F32),
        pltpu.VMEM((tile, CONV_WIDTH), F32),
    ]
    return _layer_call(_layer0_kernel, "layer0", x, front_args, front_specs, front_scratch,
                       w_out, ffn_g, w_gate, w_up, w_down, final_g)


def _layer1_kernel(x_ref, xp_ref, g_ref, win_ref, wpool_ref, pscale_ref, lg_ref, lb_ref, ws_ref, bs_ref,
                   wout_ref, fg_ref, wg_ref, wu_ref, wd_ref, ng_ref, o_ref,
                   hn_ref, z_ref, pbuf_ref, u_ref, v_ref, wsm_ref, cat_ref, h_ref, hb_ref, act_ref,
                   *, tiles_per_seq):
    i = pl.program_id(0)
    t = i % tiles_per_seq
    slot = i % 2
    tile = x_ref.shape[0]

    @pl.when(i == 0)
    def _():
        cat_ref[...] = jnp.zeros(cat_ref.shape, BF16)

    @pl.when(t == 0)
    def _():
        pbuf_ref[0:POOL_HALO, :] = jnp.zeros((POOL_HALO, POOL_WIDTH), F32)

    @pl.when(t > 0)
    def _():
        pbuf_ref[0:POOL_HALO, :] = pbuf_ref[tile:tile + POOL_HALO, :]

    front = []

    def norm():
        hn_ref[...] = _rms(x_ref[...], g_ref[...]).astype(BF16)
        ti = lax.broadcasted_iota(jnp.int32, (SGU_CHUNK, SGU_CHUNK), 0)
        si = lax.broadcasted_iota(jnp.int32, (SGU_CHUNK, SGU_CHUNK), 1)
        for gi in range(SGU_HEADS):
            wsm_ref[gi] = jnp.where(ti >= si, ws_ref[gi], 0.0).astype(BF16)

    front.append((550, norm))

    def in_proj(c):
        cs = _cols(c, IN1_WIDTH)
        z_ref[:, cs] = jnp.dot(hn_ref[...], win_ref[:, cs], preferred_element_type=F32)

    for c in range(_n_chunks(IN1_WIDTH)):
        front.append((512, functools.partial(in_proj, c)))

    def pool_fill():
        pbuf_ref[POOL_HALO:POOL_HALO + tile, :] = z_ref[:, 0:POOL_WIDTH]

    front.append((70, pool_fill))

    def pool(gi, w):
        cs = slice(gi * POOL_GROUP, (gi + 1) * POOL_GROUP)
        pos = t * tile + lax.broadcasted_iota(jnp.int32, (tile, POOL_GROUP), 0)
        cur = pbuf_ref[POOL_HALO:POOL_HALO + tile, cs]
        s = pbuf_ref[POOL_HALO - (w - 1):POOL_HALO + tile, cs]
        span = 1
        while span < w:
            n = s.shape[0] - span
            s = s[span:span + n] + s[0:n]
            span *= 2
        cnt = jnp.minimum(pos + 1, w).astype(F32)
        pooled = s / cnt - cur
        y = jnp.dot(pooled.astype(BF16), wpool_ref[gi], preferred_element_type=F32) * pscale_ref[:, cs]
        cat_ref[slot, :, cs] = y.astype(BF16)

    for gi, w in enumerate(POOL_WINDOWS):
        front.append((250, functools.partial(pool, gi, w)))

    def gate_prep(c):
        rs = slice(c * SGU_CHUNK, (c + 1) * SGU_CHUNK)
        zz = _gelu_tanh(z_ref[rs, POOL_WIDTH:POOL_WIDTH + 2 * SGU_WIDTH])
        u_ref[rs, :] = zz[:, 0:SGU_WIDTH]
        v_ref[rs, :] = _layer_norm(zz[:, SGU_WIDTH:2 * SGU_WIDTH], lg_ref[...], lb_ref[...]).astype(BF16)

    def gate_mix(gi):
        cs = slice(gi * SGU_HEAD_DIM, (gi + 1) * SGU_HEAD_DIM)
        n_chunks = tile // SGU_CHUNK
        v_all = jnp.concatenate([v_ref[c * SGU_CHUNK:(c + 1) * SGU_CHUNK, cs] for c in range(n_chunks)], axis=1)
        mixed_all = jnp.dot(wsm_ref[gi], v_all, preferred_element_type=F32)
        for c in range(n_chunks):
            rs = slice(c * SGU_CHUNK, (c + 1) * SGU_CHUNK)
            mixed = mixed_all[:, c * SGU_HEAD_DIM:(c + 1) * SGU_HEAD_DIM] + bs_ref[:, cs]
            cat_ref[slot, rs, POOL_WIDTH + gi * SGU_HEAD_DIM:POOL_WIDTH + (gi + 1) * SGU_HEAD_DIM] = (
                u_ref[rs, cs] * mixed).astype(BF16)

    for c in range(tile // SGU_CHUNK):
        front.append((450, functools.partial(gate_prep, c)))
    for gi in range(SGU_HEADS):
        front.append((520, functools.partial(gate_mix, gi)))

    back = _back_stream(_Ties(), xp_ref, cat_ref.at[1 - slot], wout_ref, fg_ref, wg_ref, wu_ref, wd_ref, ng_ref,
                        o_ref, h_ref, hb_ref, act_ref, final_norm=True)
    _emit_interleaved(front, back, front_end=0.5)


def _layer1(x, g, w_in, w_pool, pool_scale, ln_g, ln_b, w_s, b_s, w_out, ffn_g, w_gate, w_up, w_down, final_g):
    tile = SEQ_TILE
    row = lambda v: v.reshape(1, -1)
    bias = jnp.repeat(b_s.T, SGU_HEAD_DIM, axis=1)
    front_args = (row(g), w_in.astype(BF16), w_pool.astype(BF16), row(pool_scale), row(ln_g), row(ln_b), w_s, bias)
    front_specs = [
        _full((1, D_MODEL)),
        _full((D_MODEL, IN1_WIDTH)),
        _full((len(POOL_WINDOWS), POOL_GROUP, POOL_GROUP)),
        _full((1, POOL_WIDTH)),
        _full((1, SGU_WIDTH)),
        _full((1, SGU_WIDTH)),
        _full((SGU_HEADS, SGU_CHUNK, SGU_CHUNK)),
        _full((SGU_CHUNK, SGU_WIDTH)),
    ]
    front_scratch = [
        pltpu.VMEM((tile, IN1_WIDTH), F32),
        pltpu.VMEM((POOL_HALO + tile, POOL_WIDTH), F32),
        pltpu.VMEM((tile, SGU_WIDTH), F32),
        pltpu.VMEM((tile, SGU_WIDTH), BF16),
        pltpu.VMEM((SGU_HEADS, SGU_CHUNK, SGU_CHUNK), BF16),
    ]
    return _layer_call(_layer1_kernel, "layer1", x, front_args, front_specs, front_scratch,
                       w_out, ffn_g, w_gate, w_up, w_down, final_g)


def kernel(x, mix_norm, a_w_in, a_b_in, a_sinks, a_conv_w, a_conv_b, a_cln_g, a_cln_b, a_w_out, c_w_in, c_w_pool, c_pool_scale, c_sln_g, c_sln_b, c_w_s, c_b_s, c_w_out, ffn_norm, ffn_w_gate, ffn_w_up, ffn_w_down, final_norm):
    h = _layer0(x, mix_norm[0], a_w_in[0], a_b_in[0], a_sinks[0], a_conv_w[0], a_conv_b[0], a_cln_g[0],
                a_cln_b[0], a_w_out[0], ffn_norm[0], ffn_w_gate[0], ffn_w_up[0], ffn_w_down[0], final_norm)
    h = _layer1(h, mix_norm[1], c_w_in[0], c_w_pool[0], c_pool_scale[0], c_sln_g[0], c_sln_b[0], c_w_s[0],
                c_b_s[0], c_w_out[0], ffn_norm[1], ffn_w_gate[1], ffn_w_up[1], ffn_w_down[1], final_norm)
    return h
```

```python
import functools

import jax
import jax.numpy as jnp
from jax import lax
from jax.experimental import pallas as pl
from jax.experimental.pallas import tpu as pltpu

F32 = jnp.float32
BF16 = jnp.bfloat16

D_MODEL = 1024
N_Q_HEADS = 8
N_KV_HEADS = 2
HEAD_DIM = 64
Q_GROUP = N_Q_HEADS // N_KV_HEADS
ATTN_BLOCK = 128
ATTN_WIDTH = N_Q_HEADS * HEAD_DIM
KV_WIDTH = N_KV_HEADS * HEAD_DIM
CONV_WIDTH = D_MODEL - ATTN_WIDTH
CONV_KERNEL = 31
IN0_WIDTH = ATTN_WIDTH + 2 * KV_WIDTH + 2 * CONV_WIDTH
POOL_WINDOWS = (2, 4, 8, 16)
POOL_WIDTH = 512
POOL_GROUP = POOL_WIDTH // len(POOL_WINDOWS)
SGU_WIDTH = 512
SGU_HEADS = 4
SGU_HEAD_DIM = SGU_WIDTH // SGU_HEADS
SGU_CHUNK = 128
IN1_WIDTH = POOL_WIDTH + 2 * SGU_WIDTH
EPS = 1e-5

LANES = 128
SUBLANES = 8
MXU_COLS = 256
CONV_HALO = 32
CONV_ROWS = 64
POOL_HALO = 16
SEQ_TILE = 512
VMEM_LIMIT = 60 * 1024 * 1024


def _rms(x, g):
    return x * lax.rsqrt(jnp.mean(x * x, axis=-1, keepdims=True) + EPS) * g


def _layer_norm(x, g, b):
    mu = jnp.mean(x, axis=-1, keepdims=True)
    xc = x - mu
    var = jnp.mean(xc * xc, axis=-1, keepdims=True)
    return xc * lax.rsqrt(var + EPS) * g + b


def _sigmoid(x):
    return 1.0 / (1.0 + jnp.exp(-x))


def _n_chunks(width):
    return -(-width // MXU_COLS)


def _cols(c, width):
    return slice(c * MXU_COLS, min((c + 1) * MXU_COLS, width))


def _gelu_tanh(x):
    c = 0.7978845608028654
    return 0.5 * x * (1.0 + jnp.tanh(c * (x + 0.044715 * (x * x * x))))


class _Ties:
    def __init__(self):
        self.pieces = []

    def add(self, value, n_pieces):
        n_tiles = value.shape[0] // SUBLANES
        per = n_tiles // n_pieces
        for p in range(n_pieces):
            word = None
            for j in range(p * per, (p + 1) * per):
                tile_bits = pltpu.bitcast(value[j * SUBLANES:(j + 1) * SUBLANES], jnp.uint32)
                word = tile_bits if word is None else word | tile_bits
            self.pieces.append(word)

    def bind(self, result):
        if not self.pieces:
            return result
        tile_rows = 2 * SUBLANES if result.dtype == BF16 else SUBLANES
        n_rt, n_lt = result.shape[0] // tile_rows, result.shape[1] // LANES
        zeros = []
        for piece in self.pieces:
            zero = lax.shift_right_logical(lax.shift_right_logical(piece, jnp.uint32(16)), jnp.uint32(16))
            zeros.append(pltpu.bitcast(zero, result.dtype))
        grid = []
        for rt in range(n_rt):
            row = [zeros[(rt * n_lt + lt) * len(zeros) // (n_rt * n_lt)] for lt in range(n_lt)]
            grid.append(jnp.concatenate(row, axis=1))
        self.pieces = []
        return result + jnp.concatenate(grid, axis=0)


def _emit_interleaved(front, back, front_end):
    streams = [front, back]
    pace = [front_end, 1.0]
    totals = [sum(c for c, _ in s) for s in streams]
    pos = [0] * len(streams)
    done = [0.0] * len(streams)
    while True:
        live = [k for k in range(len(streams)) if pos[k] < len(streams[k])]
        if not live:
            return
        k = min(live, key=lambda j: done[j] / totals[j] * pace[j])
        cost, thunk = streams[k][pos[k]]
        thunk()
        pos[k] += 1
        done[k] += cost


def _back_stream(ties, xp_ref, cat_view, wout_ref, fg_ref, wg_ref, wu_ref, wd_ref, ng_ref, o_ref,
                 h_ref, hb_ref, act_ref, final_norm):
    units = []
    d = D_MODEL
    d_ff = wg_ref.shape[1]

    def out_proj(c):
        cs = _cols(c, d)
        h_ref[:, cs] = xp_ref[:, cs] + jnp.dot(cat_view[...], wout_ref[:, cs], preferred_element_type=F32)

    for c in range(_n_chunks(d)):
        units.append((512, functools.partial(out_proj, c)))

    def norm():
        hb_ref[...] = _rms(h_ref[...], fg_ref[...]).astype(BF16)

    units.append((500, norm))

    def gate_up(c):
        cs = _cols(c, d_ff)
        hb = hb_ref[...]
        gate = jnp.dot(hb, wg_ref[:, cs], preferred_element_type=F32)
        up = jnp.dot(hb, wu_ref[:, cs], preferred_element_type=F32)
        act_ref[:, cs] = ties.bind((gate * _sigmoid(gate) * up).astype(BF16))

    for c in range(_n_chunks(d_ff)):
        units.append((1024, functools.partial(gate_up, c)))

    def down(c):
        cs = _cols(c, d)
        y = ties.bind(h_ref[:, cs] + jnp.dot(act_ref[...], wd_ref[:, cs], preferred_element_type=F32))
        if final_norm:
            h_ref[:, cs] = y
        else:
            o_ref[:, cs] = y

    for c in range(_n_chunks(d)):
        units.append((1408, functools.partial(down, c)))

    if final_norm:
        def last():
            o_ref[...] = _rms(h_ref[...], ng_ref[...])

        units.append((500, last))
    return units


def _layer0_kernel(x_ref, xp_ref, g_ref, win_ref, bin_ref, sink_ref, cw_ref, cb_ref, lg_ref, lb_ref,
                   wout_ref, fg_ref, wg_ref, wu_ref, wd_ref, ng_ref, o_ref,
                   hn_ref, z_ref, kd_ref, vd_ref, hbuf_ref, conv_ref, cat_ref, h_ref, hb_ref, act_ref,
                   *, tiles_per_seq):
    i = pl.program_id(0)
    new = i % 2
    old = 1 - new
    t_new = i % tiles_per_seq
    tile = x_ref.shape[0]
    blk = ATTN_BLOCK
    ties = _Ties()

    @pl.when(i == 0)
    def _():
        cat_ref[...] = jnp.zeros(cat_ref.shape, BF16)

    @pl.when(t_new == 0)
    def _():
        kd_ref[:, 0:blk, :] = jnp.zeros((N_KV_HEADS, blk, LANES), BF16)
        vd_ref[:, 0:blk, :] = jnp.zeros((N_KV_HEADS, blk, 2 * LANES), BF16)

    @pl.when(t_new > 0)
    def _():
        kd_ref[:, 0:blk, :] = kd_ref[:, tile:tile + blk, :]
        vd_ref[:, 0:blk, :] = vd_ref[:, tile:tile + blk, :]

    @pl.when(t_new == 0)
    def _():
        hbuf_ref[0:CONV_HALO, :] = jnp.zeros((CONV_HALO, CONV_WIDTH), F32)

    @pl.when(t_new > 0)
    def _():
        hbuf_ref[0:CONV_HALO, :] = hbuf_ref[tile:tile + CONV_HALO, :]

    front = []

    def norm():
        hn_ref[...] = _rms(x_ref[...], g_ref[...]).astype(BF16)

    def in_proj(c):
        cs = _cols(c, IN0_WIDTH)
        z_ref[:, cs] = jnp.dot(hn_ref[...], win_ref[:, cs], preferred_element_type=F32) + bin_ref[:, cs]

    def kv_dup():
        lo_t = lax.broadcasted_iota(jnp.int32, (tile, LANES), 1) < HEAD_DIM
        zk = z_ref[:, ATTN_WIDTH:ATTN_WIDTH + KV_WIDTH]
        zv = z_ref[:, ATTN_WIDTH + KV_WIDTH:ATTN_WIDTH + 2 * KV_WIDTH]
        zk_sw = pltpu.roll(zk, HEAD_DIM, 1)
        zv_sw = pltpu.roll(zv, HEAD_DIM, 1)
        kd_ref[0, blk:blk + tile, :] = jnp.where(lo_t, zk, zk_sw).astype(BF16)
        kd_ref[1, blk:blk + tile, :] = jnp.where(lo_t, zk_sw, zk).astype(BF16)
        vd_ref[0, blk:blk + tile, 0:LANES] = jnp.where(lo_t, zv, zv_sw).astype(BF16)
        vd_ref[1, blk:blk + tile, 0:LANES] = jnp.where(lo_t, zv_sw, zv).astype(BF16)
        vd_ref[:, blk:blk + tile, LANES:2 * LANES] = jnp.ones((N_KV_HEADS, tile, LANES), BF16)

    def glu(c):
        rs = slice(c * CONV_ROWS, (c + 1) * CONV_ROWS)
        c0 = ATTN_WIDTH + 2 * KV_WIDTH
        a = z_ref[rs, c0:c0 + CONV_WIDTH]
        gate = z_ref[rs, c0 + CONV_WIDTH:c0 + 2 * CONV_WIDTH]
        hbuf_ref[CONV_HALO + c * CONV_ROWS:CONV_HALO + (c + 1) * CONV_ROWS, :] = a * _sigmoid(gate)

    rows = Q_GROUP * blk
    scale = HEAD_DIM ** -0.5

    def attn_block(b, kh):
        qi = lax.broadcasted_iota(jnp.int32, (rows, blk), 0) % blk
        col = lax.broadcasted_iota(jnp.int32, (rows, blk), 1)
        from_prev = col > qi
        lo = lax.broadcasted_iota(jnp.int32, (blk, LANES), 1) < HEAD_DIM
        rs = slice(b * blk, (b + 1) * blk)
        parts = []
        for j in range(Q_GROUP // 2):
            c0 = kh * Q_GROUP * HEAD_DIM + j * LANES
            qp = z_ref[rs, c0:c0 + LANES] * scale
            parts.append(jnp.where(lo, qp, 0.0))
            parts.append(jnp.where(lo, 0.0, qp))
        qs = jnp.concatenate(parts, axis=0).astype(BF16)
        kw = kd_ref[kh, b * blk:(b + 2) * blk, :]
        both = lax.dot_general(qs, kw, (((1,), (1,)), ((), ())), preferred_element_type=F32)
        logits = jnp.where(from_prev, both[:, 0:blk], both[:, blk:2 * blk])
        if b == 0:
            logits = jnp.where(from_prev & (t_new == 0), -jnp.inf, logits)
        sink = sink_ref[kh]
        m = jnp.maximum(jnp.max(logits, axis=-1, keepdims=True), sink)
        p = jnp.exp(logits - m)
        pw = jnp.concatenate([jnp.where(from_prev, p, 0.0), jnp.where(from_prev, 0.0, p)], axis=1)
        vw = vd_ref[kh, b * blk:(b + 2) * blk, :]
        ext = jnp.dot(pw.astype(BF16), vw, preferred_element_type=F32)
        o = ext[:, 0:LANES] / (ext[:, LANES:2 * LANES] + jnp.exp(sink - m))
        for j in range(Q_GROUP // 2):
            pair = jnp.where(lo, o[2 * j * blk:(2 * j + 1) * blk], o[(2 * j + 1) * blk:(2 * j + 2) * blk])
            c0 = kh * Q_GROUP * HEAD_DIM + j * LANES
            cat_ref[new, rs, c0:c0 + LANES] = pair.astype(BF16)

    def conv_slab(c, cg):
        rs = slice(c * CONV_ROWS, (c + 1) * CONV_ROWS)
        ls = slice(cg * LANES, (cg + 1) * LANES)
        depth = (CONV_KERNEL - 1) // SUBLANES
        top = CONV_HALO + c * CONV_ROWS - (depth + 1) * SUBLANES
        win = hbuf_ref[top:top + (depth + 1) * SUBLANES + CONV_ROWS, ls]
        acc = None
        for r in range(SUBLANES):
            pad = SUBLANES if r else 0
            n = CONV_ROWS + pad
            part = None
            for a in range((CONV_KERNEL - 1 - r) // SUBLANES + 1):
                k = CONV_KERNEL - 1 - (SUBLANES * a + r)
                src = (depth + 1 - a) * SUBLANES - pad
                term = cw_ref[k:k + 1, ls] * win[src:src + n]
                part = term if part is None else part + term
            if r:
                part = part[SUBLANES - r:SUBLANES - r + CONV_ROWS]
            acc = part if acc is None else acc + part
        conv_ref[rs, ls] = acc + cb_ref[:, ls]
        ties.add(acc, 4)

    def conv_norm(c):
        rs = slice(c * CONV_ROWS, (c + 1) * CONV_ROWS)
        y = _layer_norm(conv_ref[rs, :], lg_ref[...], lb_ref[...])
        y = y * _sigmoid(y)
        cat_ref[new, rs, ATTN_WIDTH:ATTN_WIDTH + CONV_WIDTH] = y.astype(BF16)
        for cg in range(CONV_WIDTH // LANES):
            ties.add(y[:, cg * LANES:(cg + 1) * LANES], 2)

    front.append((500, norm))
    for c in range(_n_chunks(IN0_WIDTH)):
        front.append((512, functools.partial(in_proj, c)))
    front.append((150, kv_dup))
    for c in range(tile // CONV_ROWS):
        front.append((60, functools.partial(glu, c)))
    steps = [(b, kh) for b in range(tile // blk) for kh in range(N_KV_HEADS)]
    slabs = [(c, cg) for c in range(tile // CONV_ROWS) for cg in range(CONV_WIDTH // LANES)]
    slabs_per_step = len(slabs) // len(steps)
    for n, (b, kh) in enumerate(steps):
        front.append((700, functools.partial(attn_block, b, kh)))
        for c, cg in slabs[n * slabs_per_step:(n + 1) * slabs_per_step]:
            front.append((170, functools.partial(conv_slab, c, cg)))
            if cg == CONV_WIDTH // LANES - 1:
                front.append((150, functools.partial(conv_norm, c)))

    back = _back_stream(ties, xp_ref, cat_ref.at[old], wout_ref, fg_ref, wg_ref, wu_ref, wd_ref, ng_ref, o_ref,
                        h_ref, hb_ref, act_ref, final_norm=False)
    _emit_interleaved(front, back, front_end=0.75)


def _full(shape):
    return pl.BlockSpec(shape, lambda *_: (0,) * len(shape))


def _layer_call(body, name, x, front_args, front_specs, front_scratch, w_out, ffn_g, w_gate, w_up, w_down,
                final_g):
    B, S, D = x.shape
    M = B * S
    tile = SEQ_TILE
    n_tiles = M // tile
    d_ff = w_gate.shape[1]
    row = lambda v: v.reshape(1, -1)
    out = pl.pallas_call(
        functools.partial(body, tiles_per_seq=S // tile),
        name=name,
        grid=(n_tiles + 1,),
        in_specs=[
            pl.BlockSpec((tile, D), lambda i: (jnp.minimum(i, n_tiles - 1), 0)),
            pl.BlockSpec((tile, D), lambda i: (jnp.maximum(i - 1, 0), 0)),
            *front_specs,
            _full((D, D)),
            _full((1, D)),
            _full((D, d_ff)),
            _full((D, d_ff)),
            _full((d_ff, D)),
            _full((1, D)),
        ],
        out_specs=pl.BlockSpec((tile, D), lambda i: (jnp.maximum(i - 1, 0), 0)),
        out_shape=jax.ShapeDtypeStruct((M, D), F32),
        scratch_shapes=[
            pltpu.VMEM((tile, D), BF16),
            *front_scratch,
            pltpu.VMEM((2, tile, D), BF16),
            pltpu.VMEM((tile, D), F32),
            pltpu.VMEM((tile, D), BF16),
            pltpu.VMEM((tile, d_ff), BF16),
        ],
        compiler_params=pltpu.CompilerParams(
            dimension_semantics=("arbitrary",), vmem_limit_bytes=VMEM_LIMIT),
    )(x.reshape(M, D), x.reshape(M, D), *front_args, w_out.astype(BF16), row(ffn_g), w_gate.astype(BF16),
      w_up.astype(BF16), w_down.astype(BF16), row(final_g))
    return out.reshape(B, S, D)


def _layer0(x, g, w_in, b_in, sinks, conv_w, conv_b, ln_g, ln_b, w_out, ffn_g, w_gate, w_up, w_down, final_g):
    tile = SEQ_TILE
    rows = Q_GROUP * ATTN_BLOCK
    row = lambda v: v.reshape(1, -1)
    sink_cols = jnp.repeat(sinks.reshape(N_KV_HEADS, Q_GROUP), ATTN_BLOCK, axis=1).reshape(N_KV_HEADS, rows, 1)
    front_args = (row(g), w_in.astype(BF16), row(b_in), sink_cols, conv_w, row(conv_b), row(ln_g), row(ln_b))
    front_specs = [
        _full((1, D_MODEL)),
        _full((D_MODEL, IN0_WIDTH)),
        _full((1, IN0_WIDTH)),
        _full((N_KV_HEADS, rows, 1)),
        _full((CONV_KERNEL, CONV_WIDTH)),
        _full((1, CONV_WIDTH)),
        _full((1, CONV_WIDTH)),
        _full((1, CONV_WIDTH)),
    ]
    front_scratch = [
        pltpu.VMEM((tile, IN0_WIDTH), F32),
        pltpu.VMEM((N_KV_HEADS, ATTN_BLOCK + tile, LANES), BF16),
        pltpu.VMEM((N_KV_HEADS, ATTN_BLOCK + tile, 2 * LANES), BF16),
        pltpu.VMEM((CONV_HALO + tile, CONV_WIDTH), F32),
        pltpu.VMEM((tile, CONV_WIDTH), F32),
    ]
    return _layer_call(_layer0_kernel, "layer0", x, front_args, front_specs, front_scratch,
                       w_out, ffn_g, w_gate, w_up, w_down, final_g)


def _layer1_kernel(x_ref, xp_ref, g_ref, win_ref, wpool_ref, pscale_ref, lg_ref, lb_ref, ws_ref, bs_ref,
                   wout_ref, fg_ref, wg_ref, wu_ref, wd_ref, ng_ref, o_ref,
                   hn_ref, z_ref, pbuf_ref, u_ref, v_ref, wsm_ref, cat_ref, h_ref, hb_ref, act_ref,
                   *, tiles_per_seq):
    i = pl.program_id(0)
    t = i % tiles_per_seq
    slot = i % 2
    tile = x_ref.shape[0]

    @pl.when(i == 0)
    def _():
        cat_ref[...] = jnp.zeros(cat_ref.shape, BF16)

    @pl.when(t == 0)
    def _():
        pbuf_ref[0:POOL_HALO, :] = jnp.zeros((POOL_HALO, POOL_WIDTH), F32)

    @pl.when(t > 0)
    def _():
        pbuf_ref[0:POOL_HALO, :] = pbuf_ref[tile:tile + POOL_HALO, :]

    front = []

    def norm():
        hn_ref[...] = _rms(x_ref[...], g_ref[...]).astype(BF16)
        ti = lax.broadcasted_iota(jnp.int32, (SGU_CHUNK, SGU_CHUNK), 0)
        si = lax.broadcasted_iota(jnp.int32, (SGU_CHUNK, SGU_CHUNK), 1)
        for gi in range(SGU_HEADS):
            wsm_ref[gi] = jnp.where(ti >= si, ws_ref[gi], 0.0).astype(BF16)

    front.append((550, norm))

    def in_proj(c):
        cs = _cols(c, IN1_WIDTH)
        z_ref[:, cs] = jnp.dot(hn_ref[...], win_ref[:, cs], preferred_element_type=F32)

    for c in range(_n_chunks(IN1_WIDTH)):
        front.append((512, functools.partial(in_proj, c)))

    def pool_fill():
        pbuf_ref[POOL_HALO:POOL_HALO + tile, :] = z_ref[:, 0:POOL_WIDTH]

    front.append((70, pool_fill))

    def pool(gi, w):
        cs = slice(gi * POOL_GROUP, (gi + 1) * POOL_GROUP)
        pos = t * tile + lax.broadcasted_iota(jnp.int32, (tile, POOL_GROUP), 0)
        cur = pbuf_ref[POOL_HALO:POOL_HALO + tile, cs]
        s = pbuf_ref[POOL_HALO - (w - 1):POOL_HALO + tile, cs]
        span = 1
        while span < w:
            n = s.shape[0] - span
            s = s[span:span + n] + s[0:n]
            span *= 2
        cnt = jnp.minimum(pos + 1, w).astype(F32)
        pooled = s / cnt - cur
        y = jnp.dot(pooled.astype(BF16), wpool_ref[gi], preferred_element_type=F32) * pscale_ref[:, cs]
        cat_ref[slot, :, cs] = y.astype(BF16)

    for gi, w in enumerate(POOL_WINDOWS):
        front.append((250, functools.partial(pool, gi, w)))

    def gate_prep(c):
        rs = slice(c * SGU_CHUNK, (c + 1) * SGU_CHUNK)
        zz = _gelu_tanh(z_ref[rs, POOL_WIDTH:POOL_WIDTH + 2 * SGU_WIDTH])
        u_ref[rs, :] = zz[:, 0:SGU_WIDTH]
        v_ref[rs, :] = _layer_norm(zz[:, SGU_WIDTH:2 * SGU_WIDTH], lg_ref[...], lb_ref[...]).astype(BF16)

    def gate_mix(gi):
        cs = slice(gi * SGU_HEAD_DIM, (gi + 1) * SGU_HEAD_DIM)
        n_chunks = tile // SGU_CHUNK
        v_all = jnp.concatenate([v_ref[c * SGU_CHUNK:(c + 1) * SGU_CHUNK, cs] for c in range(n_chunks)], axis=1)
        mixed_all = jnp.dot(wsm_ref[gi], v_all, preferred_element_type=F32)
        for c in range(n_chunks):
            rs = slice(c * SGU_CHUNK, (c + 1) * SGU_CHUNK)
            mixed = mixed_all[:, c * SGU_HEAD_DIM:(c + 1) * SGU_HEAD_DIM] + bs_ref[:, cs]
            cat_ref[slot, rs, POOL_WIDTH + gi * SGU_HEAD_DIM:POOL_WIDTH + (gi + 1) * SGU_HEAD_DIM] = (
                u_ref[rs, cs] * mixed).astype(BF16)

    for c in range(tile // SGU_CHUNK):
        front.append((450, functools.partial(gate_prep, c)))
    for gi in range(SGU_HEADS):
        front.append((520, functools.partial(gate_mix, gi)))

    back = _back_stream(_Ties(), xp_ref, cat_ref.at[1 - slot], wout_ref, fg_ref, wg_ref, wu_ref, wd_ref, ng_ref,
                        o_ref, h_ref, hb_ref, act_ref, final_norm=True)
    _emit_interleaved(front, back, front_end=0.5)


def _layer1(x, g, w_in, w_pool, pool_scale, ln_g, ln_b, w_s, b_s, w_out, ffn_g, w_gate, w_up, w_down, final_g):
    tile = SEQ_TILE
    row = lambda v: v.reshape(1, -1)
    bias = jnp.repeat(b_s.T, SGU_HEAD_DIM, axis=1)
    front_args = (row(g), w_in.astype(BF16), w_pool.astype(BF16), row(pool_scale), row(ln_g), row(ln_b), w_s, bias)
    front_specs = [
        _full((1, D_MODEL)),
        _full((D_MODEL, IN1_WIDTH)),
        _full((len(POOL_WINDOWS), POOL_GROUP, POOL_GROUP)),
        _full((1, POOL_WIDTH)),
        _full((1, SGU_WIDTH)),
        _full((1, SGU_WIDTH)),
        _full((SGU_HEADS, SGU_CHUNK, SGU_CHUNK)),
        _full((SGU_CHUNK, SGU_WIDTH)),
    ]
    front_scratch = [
        pltpu.VMEM((tile, IN1_WIDTH), F32),
        pltpu.VMEM((POOL_HALO + tile, POOL_WIDTH), F32),
        pltpu.VMEM((tile, SGU_WIDTH), F32),
        pltpu.VMEM((tile, SGU_WIDTH), BF16),
        pltpu.VMEM((SGU_HEADS, SGU_CHUNK, SGU_CHUNK), BF16),
    ]
    return _layer_call(_layer1_kernel, "layer1", x, front_args, front_specs, front_scratch,
                       w_out, ffn_g, w_gate, w_up, w_down, final_g)


def kernel(x, mix_norm, a_w_in, a_b_in, a_sinks, a_conv_w, a_conv_b, a_cln_g, a_cln_b, a_w_out, c_w_in, c_w_pool, c_pool_scale, c_sln_g, c_sln_b, c_w_s, c_b_s, c_w_out, ffn_norm, ffn_w_gate, ffn_w_up, ffn_w_down, final_norm):
    h = _layer0(x, mix_norm[0], a_w_in[0], a_b_in[0], a_sinks[0], a_conv_w[0], a_conv_b[0], a_cln_g[0],
                a_cln_b[0], a_w_out[0], ffn_norm[0], ffn_w_gate[0], ffn_w_up[0], ffn_w_down[0], final_norm)
    h = _layer1(h, mix_norm[1], c_w_in[0], c_w_pool[0], c_pool_scale[0], c_sln_g[0], c_sln_b[0], c_w_s[0],
                c_b_s[0], c_w_out[0], ffn_norm[1], ffn_w_gate[1], ffn_w_up[1], ffn_w_down[1], final_norm)
    return h
```
